```python
import jax, jax.numpy as jnp
from jax import lax
import numpy as np

D_MODEL = 2048
BATCH = 16
SEQ = 2048
DEPTH = 2

GRID_W = 64
CTX_LEN = 256
N_MOD = 6
NORM_EPS = 1e-6

RWKV_HEADS = 16
RWKV_HEAD_DIM = 64
D_RWKV = RWKV_HEADS * RWKV_HEAD_DIM
LORA_W = 96
LORA_A = 96
LORA_G = 256
RWKV_GN_EPS = 64e-5
D_SHIFT = 3 * D_RWKV + LORA_W + LORA_A + LORA_G
RWKV_SPLITS = (D_RWKV, 2 * D_RWKV, 3 * D_RWKV, 3 * D_RWKV + LORA_W, 3 * D_RWKV + LORA_W + LORA_A)

GMLP_GROUPS = 8
GMLP_GROUP_DIM = 128
D_GMLP = GMLP_GROUPS * GMLP_GROUP_DIM
CHUNK = 128
D_EVEN_IN = D_SHIFT + 2 * D_GMLP

ATT_HEADS = 16
ATT_KV_HEADS = 4
ATT_GROUP = ATT_HEADS // ATT_KV_HEADS
ATT_HEAD_DIM = 128
ROPE_THETA = 10000.0
Q_BLOCK = 128
D_ODD_IN = (ATT_HEADS + 2 * ATT_KV_HEADS) * ATT_HEAD_DIM

N_EXPERTS = 32
TOP_K = 4
D_FF_EXPERT = D_MODEL
SWIGLU_ALPHA = 1.702
SWIGLU_LIMIT = 7.0

kernel_name = 'hybrid_rwkv7_gmlp_gqa_moe_dit'


def rmsnorm(x, g):
    xf = x.astype(jnp.float32)
    y = xf * lax.rsqrt(jnp.mean(xf * xf, axis=-1, keepdims=True) + NORM_EPS)
    return (y * g).astype(x.dtype)


def layernorm(x, g, b, eps=1e-5):
    xf = x.astype(jnp.float32)
    mu = jnp.mean(xf, axis=-1, keepdims=True)
    var = jnp.mean(jnp.square(xf - mu), axis=-1, keepdims=True)
    return ((xf - mu) * lax.rsqrt(var + eps) * g + b).astype(x.dtype)


def adaln(cvec, w_mod, b_mod):
    m = jnp.dot(jax.nn.silu(cvec), w_mod) + b_mod
    return jnp.split(m, N_MOD, axis=-1)


def modulate(h, shift, scale):
    return h * (1.0 + scale) + shift


def centred_shift(y, mu):
    pad = jnp.pad(y, ((0, 0), (1, 1), (0, 0)))
    nb = 0.5 * (pad[:, :-2] + pad[:, 2:])
    return y + mu * (nb - y)


def rwkv_prepare(y, mu, w0, w_up, a0, a_up, g_up, k_k, k_a):
    y = centred_shift(y.astype(jnp.float32), mu)
    r, k, v, xw, xa, xg = jnp.split(y, RWKV_SPLITS, axis=-1)
    B, T, _ = y.shape
    heads = lambda t: t.reshape(B, T, RWKV_HEADS, RWKV_HEAD_DIM)
    kk = heads(k * k_k)
    kk = kk * lax.rsqrt(jnp.sum(kk * kk, axis=-1, keepdims=True) + 1e-12)
    g = jnp.dot(jax.nn.sigmoid(xg), g_up)
    decays, keys, rates = [], [], []
    for d in range(2):
        w_log = -jax.nn.softplus(-(w0[d] + jnp.dot(jnp.tanh(xw), w_up[d]))) - 0.5
        a = jax.nn.sigmoid(a0[d] + jnp.dot(xa, a_up[d]))
        decays.append(heads(jnp.exp(-jnp.exp(w_log))))
        keys.append(heads(k * (1.0 + (a - 1.0) * k_a)))
        rates.append(heads(a))
    return heads(r), heads(v), kk, g, decays, keys, rates


def wkv7_scan(r, decay, k, v, kk, a, s0, reverse):
    def step(s, inp):
        r_t, w_t, k_t, v_t, kk_t, a_t = inp
        sa = jnp.einsum('bhij,bhj->bhi', s, kk_t)
        s = (s * w_t[:, :, None, :] - sa[..., None] * (kk_t * a_t)[:, :, None, :]
             + v_t[..., None] * k_t[:, :, None, :])
        return s, jnp.einsum('bhij,bhj->bhi', s, r_t)
    xs = tuple(jnp.moveaxis(t, 1, 0) for t in (r, decay, k, v, kk, a))
    s_final, y = lax.scan(step, s0, xs, reverse=reverse)
    return jnp.moveaxis(y, 0, 1), s_final


def run_direction(p, d, s_init):
    r, v, kk, _, decays, keys, rates = p
    return wkv7_scan(r, decays[d], keys[d], v, kk, rates[d], s_init, d == 1)


def head_groupnorm(x, g, b):
    B, T, H, N = x.shape
    mu = jnp.mean(x, axis=-1, keepdims=True)
    var = jnp.mean(jnp.square(x - mu), axis=-1, keepdims=True)
    return ((x - mu) * lax.rsqrt(var + RWKV_GN_EPS)).reshape(B, T, H * N) * g + b


def rwkv_output(p, wkv, r_k, lnx_g, lnx_b):
    r, v, _, g, _, keys, _ = p
    B, T = r.shape[:2]
    bonus = (jnp.sum(r * keys[0] * r_k, axis=-1, keepdims=True)
             + jnp.sum(r * keys[1] * r_k, axis=-1, keepdims=True)) * v
    return (head_groupnorm(wkv, lnx_g, lnx_b) + bonus.reshape(B, T, D_RWKV)) * g


def chunk_gmlp(y, ln_g, ln_b, ws, bs):
    B, T, _ = y.shape
    u, v = jnp.split(jax.nn.gelu(y, approximate=False), 2, axis=-1)
    v = layernorm(v, ln_g, ln_b)
    vc = v.reshape(B, T // CHUNK, CHUNK, GMLP_GROUPS, GMLP_GROUP_DIM)
    mixed = jnp.einsum('gij,bnjgc->bnigc', ws, vc) + bs.T[:, :, None]
    return u * mixed.reshape(B, T, D_GMLP)


def even_mixer(nx, nc, w_in, mu, w0, w_up, a0, a_up, g_up, k_k, k_a, r_k, lnx_g, lnx_b,
               v_ln_g, v_ln_b, ws, bs, w_out, need_ctx):
    rwkv_p = (mu, w0, w_up, a0, a_up, g_up, k_k, k_a)
    yx = jnp.dot(nx, w_in)
    yc = jnp.dot(nc, w_in)
    px = rwkv_prepare(yx[..., :D_SHIFT], *rwkv_p)
    pc = rwkv_prepare(yc[..., :D_SHIFT], *rwkv_p)
    s0 = jnp.zeros((nx.shape[0], RWKV_HEADS, RWKV_HEAD_DIM, RWKV_HEAD_DIM), jnp.float32)
    oc_f, sc_f = run_direction(pc, 0, s0)
    oc_b, sc_b = run_direction(pc, 1, s0)
    ox_f, _ = run_direction(px, 0, sc_f)
    ox_b, _ = run_direction(px, 1, sc_b)
    rx = rwkv_output(px, ox_f + ox_b, r_k, lnx_g, lnx_b).astype(nx.dtype)
    gx = chunk_gmlp(yx[..., D_SHIFT:], v_ln_g, v_ln_b, ws, bs)
    mix_x = jnp.dot(jnp.concatenate([rx, gx], axis=-1), w_out)
    if not need_ctx:
        return mix_x, None
    rc = rwkv_output(pc, oc_f + oc_b, r_k, lnx_g, lnx_b).astype(nc.dtype)
    gc = chunk_gmlp(yc[..., D_SHIFT:], v_ln_g, v_ln_b, ws, bs)
    return mix_x, jnp.dot(jnp.concatenate([rc, gc], axis=-1), w_out)


def axial_rope_tables(rows):
    row = jnp.repeat(jnp.arange(rows, dtype=jnp.float32), GRID_W)
    col = jnp.tile(jnp.arange(GRID_W, dtype=jnp.float32), rows)
    n_freq = ATT_HEAD_DIM // 4
    inv = jnp.power(jnp.float32(ROPE_THETA), -jnp.arange(n_freq, dtype=jnp.float32) / n_freq)
    ang_r = (row[:, None] * inv)[:, None, :]
    ang_c = (col[:, None] * inv)[:, None, :]
    return jnp.cos(ang_r), jnp.sin(ang_r), jnp.cos(ang_c), jnp.sin(ang_c)


def rotate(x, cos, sin):
    x1, x2 = jnp.split(x, 2, axis=-1)
    return jnp.concatenate([x1 * cos - x2 * sin, x1 * sin + x2 * cos], axis=-1)


def apply_axial_rope(x, rope):
    cos_r, sin_r, cos_c, sin_c = rope
    xr, xc = jnp.split(x.astype(jnp.float32), 2, axis=-1)
    return jnp.concatenate([rotate(xr, cos_r, sin_r), rotate(xc, cos_c, sin_c)], axis=-1).astype(x.dtype)


def split_qkv(h, w_in, q_g, k_g):
    B, T, _ = h.shape
    q, k, v = jnp.split(jnp.dot(h, w_in),
                        (ATT_HEADS * ATT_HEAD_DIM, (ATT_HEADS + ATT_KV_HEADS) * ATT_HEAD_DIM), axis=-1)
    q = rmsnorm(q.reshape(B, T, ATT_HEADS, ATT_HEAD_DIM), q_g)
    k = rmsnorm(k.reshape(B, T, ATT_KV_HEADS, ATT_HEAD_DIM), k_g)
    return q, k, v.reshape(B, T, ATT_KV_HEADS, ATT_HEAD_DIM)


def gqa_attend(q, k, v):
    s = jnp.einsum('bqhgd,bkhd->bhgqk', q.astype(jnp.float32), k.astype(jnp.float32)) * ATT_HEAD_DIM ** -0.5
    p = jax.nn.softmax(s, axis=-1)
    return jnp.einsum('bhgqk,bkhd->bqhgd', p.astype(v.dtype), v)


def odd_mixer(nx, nc, w_in, q_g, k_g, w_out, rope, need_ctx):
    B, S, _ = nx.shape
    L = nc.shape[1]
    qx, kx, vx = split_qkv(nx, w_in, q_g, k_g)
    qx = apply_axial_rope(qx, rope)
    kx = apply_axial_rope(kx, rope)
    qc, kc, vc = split_qkv(nc, w_in, q_g, k_g)
    keys = jnp.concatenate([kc, kx], axis=1)
    vals = jnp.concatenate([vc, vx], axis=1)
    qb = qx.reshape(B, S // Q_BLOCK, Q_BLOCK, ATT_KV_HEADS, ATT_GROUP, ATT_HEAD_DIM).swapaxes(0, 1)
    ob = lax.map(lambda q_blk: gqa_attend(q_blk, keys, vals), qb)
    ox = ob.swapaxes(0, 1).reshape(B, S, ATT_HEADS * ATT_HEAD_DIM)
    mix_x = jnp.dot(ox, w_out)
    if not need_ctx:
        return mix_x, None
    oc = gqa_attend(qc.reshape(B, L, ATT_KV_HEADS, ATT_GROUP, ATT_HEAD_DIM), kc, vc)
    return mix_x, jnp.dot(oc.reshape(B, L, ATT_HEADS * ATT_HEAD_DIM), w_out)


def moe(h, router_w, router_b, w1, b1, w2, b2):
    logits = (jnp.dot(h, router_w) + router_b).astype(jnp.float32)
    vals, idx = lax.top_k(logits, TOP_K)
    wts = jax.nn.softmax(vals, axis=-1)
    gates = jnp.sum(jax.nn.one_hot(idx, N_EXPERTS, dtype=jnp.float32) * wts[..., None], axis=1).astype(h.dtype)
    out = jnp.zeros_like(h)
    for e in range(N_EXPERTS):
        glu, lin = jnp.split(jnp.dot(h, w1[e]) + b1[e], 2, axis=-1)
        glu = jnp.minimum(glu, SWIGLU_LIMIT)
        lin = jnp.clip(lin, -SWIGLU_LIMIT, SWIGLU_LIMIT)
        act = glu * jax.nn.sigmoid(SWIGLU_ALPHA * glu) * (lin + 1.0)
        out = out + gates[:, e:e + 1] * (jnp.dot(act, w2[e]) + b2[e])
    return out


def setup_inputs(seed: int = 0) -> dict:
    key = jax.random.key(seed)
    keys = iter(jax.random.split(key, 48))
    f32 = jnp.float32

    def normal(shape, scale):
        return scale * jax.random.normal(next(keys), shape, f32)

    def dense(shape, fan_in, gain=1.0):
        return normal(shape, gain * fan_in ** -0.5)

    def ones_noise(shape):
        return 1.0 + normal(shape, 0.05)

    n_even = (DEPTH + 1) // 2
    n_odd = DEPTH // 2
    D = D_MODEL
    return {
        'x': normal((BATCH, SEQ, D), 1.0),
        'c': normal((BATCH, D), 1.0),
        'ctx': normal((BATCH, CTX_LEN, D), 1.0),
        'c_ctx': normal((D,), 1.0),
        'norm1_g': ones_noise((DEPTH, D)),
        'norm2_g': ones_noise((DEPTH, D)),
        'w_mod': dense((DEPTH, D, N_MOD * D), D, 0.5),
        'b_mod': normal((DEPTH, N_MOD * D), 0.02),
        'e_w_in': dense((n_even, D, D_EVEN_IN), D),
        'e_mu': jax.random.uniform(next(keys), (n_even, D_SHIFT), f32),
        'e_w0': -2.0 + normal((n_even, 2, D_RWKV), 0.5),
        'e_w_up': dense((n_even, 2, LORA_W, D_RWKV), LORA_W, 0.5),
        'e_a0': normal((n_even, 2, D_RWKV), 0.3),
        'e_a_up': dense((n_even, 2, LORA_A, D_RWKV), LORA_A, 0.5),
        'e_g_up': dense((n_even, LORA_G, D_RWKV), LORA_G),
        'e_k_k': 0.85 + normal((n_even, D_RWKV), 0.05),
        'e_k_a': ones_noise((n_even, D_RWKV)),
        'e_r_k': normal((n_even, RWKV_HEADS, RWKV_HEAD_DIM), 0.3),
        'e_lnx_g': ones_noise((n_even, D_RWKV)),
        'e_lnx_b': normal((n_even, D_RWKV), 0.02),
        'e_v_ln_g': ones_noise((n_even, D_GMLP)),
        'e_v_ln_b': normal((n_even, D_GMLP), 0.02),
        'e_ws': dense((n_even, GMLP_GROUPS, CHUNK, CHUNK), CHUNK, 0.5),
        'e_bs': 1.0 + normal((n_even, GMLP_GROUPS, CHUNK), 0.1),
        'e_w_out': dense((n_even, D_RWKV + D_GMLP, D), D_RWKV + D_GMLP),
        'o_w_in': dense((n_odd, D, D_ODD_IN), D),
        'o_q_g': ones_noise((n_odd, ATT_HEAD_DIM)),
        'o_k_g': ones_noise((n_odd, ATT_HEAD_DIM)),
        'o_w_out': dense((n_odd, ATT_HEADS * ATT_HEAD_DIM, D), ATT_HEADS * ATT_HEAD_DIM),
        'router_w': dense((DEPTH, D, N_EXPERTS), D),
        'router_b': normal((DEPTH, N_EXPERTS), 0.01),
        'moe_w1': dense((DEPTH, N_EXPERTS, D, 2 * D_FF_EXPERT), D),
        'moe_b1': normal((DEPTH, N_EXPERTS, 2 * D_FF_EXPERT), 0.02),
        'moe_w2': dense((DEPTH, N_EXPERTS, D_FF_EXPERT, D), D_FF_EXPERT),
        'moe_b2': normal((DEPTH, N_EXPERTS, D), 0.02),
        'final_g': ones_noise((D,)),
    }


def reference(x, c, ctx, c_ctx, norm1_g, norm2_g, w_mod, b_mod, e_w_in, e_mu, e_w0, e_w_up,
              e_a0, e_a_up, e_g_up, e_k_k, e_k_a, e_r_k, e_lnx_g, e_lnx_b, e_v_ln_g, e_v_ln_b,
              e_ws, e_bs, e_w_out, o_w_in, o_q_g, o_k_g, o_w_out, router_w, router_b,
              moe_w1, moe_b1, moe_w2, moe_b2, final_g):
    B, S, D = x.shape
    L = ctx.shape[1]
    rows = S // GRID_W
    rope = axial_rope_tables(rows)
    h_lat, h_ctx = x, ctx
    for i in range(DEPTH):
        need_ctx = i < DEPTH - 1
        j = i // 2
        m_lat = adaln(c[:, None, :], w_mod[i], b_mod[i])
        m_ctx = adaln(c_ctx[None, None, :], w_mod[i], b_mod[i])
        nx = modulate(rmsnorm(h_lat, norm1_g[i]), m_lat[0], m_lat[1])
        nc = modulate(rmsnorm(h_ctx, norm1_g[i]), m_ctx[0], m_ctx[1])
        if i % 2 == 0:
            mx, mc = even_mixer(nx, nc, e_w_in[j], e_mu[j], e_w0[j], e_w_up[j], e_a0[j], e_a_up[j],
                                e_g_up[j], e_k_k[j], e_k_a[j], e_r_k[j], e_lnx_g[j], e_lnx_b[j],
                                e_v_ln_g[j], e_v_ln_b[j], e_ws[j], e_bs[j], e_w_out[j], need_ctx)
        else:
            mx, mc = odd_mixer(nx, nc, o_w_in[j], o_q_g[j], o_k_g[j], o_w_out[j], rope, need_ctx)
        h_lat = h_lat + m_lat[2] * mx
        nx = modulate(rmsnorm(h_lat, norm2_g[i]), m_lat[3], m_lat[4])
        if need_ctx:
            h_ctx = h_ctx + m_ctx[2] * mc
            nc = modulate(rmsnorm(h_ctx, norm2_g[i]), m_ctx[3], m_ctx[4])
            tokens = jnp.concatenate([nx.reshape(B * S, D), nc.reshape(B * L, D)], axis=0)
        else:
            tokens = nx.reshape(B * S, D)
        f = moe(tokens, router_w[i], router_b[i], moe_w1[i], moe_b1[i], moe_w2[i], moe_b2[i])
        h_lat = h_lat + m_lat[5] * f[:B * S].reshape(B, S, D)
        if need_ctx:
            h_ctx = h_ctx + m_ctx[5] * f[B * S:].reshape(B, L, D)
    return rmsnorm(h_lat, final_g)
```

```python
import functools

import jax
import jax.numpy as jnp
from jax import lax
from jax.experimental import pallas as pl
from jax.experimental.pallas import tpu as pltpu

F32 = jnp.float32
BF16 = jnp.bfloat16

NORM_EPS = 1e-6
N_MOD = 6
RWKV_HEAD_DIM = 64
RWKV_GN_EPS = 64e-5
GMLP_LN_EPS = 1e-5
ATT_HEAD_DIM = 128
GRID_W = 64
ROPE_THETA = 10000.0
TOP_K = 4
SWIGLU_ALPHA = 1.702
SWIGLU_LIMIT = 7.0

LANES = 128
VMEM_LIMIT_BYTES = 56 * 1024 * 1024

WKV_CHUNK = 64


def _params(*sem):
    return pltpu.CompilerParams(dimension_semantics=sem, vmem_limit_bytes=VMEM_LIMIT_BYTES)


def _dot(a, b):
    return jnp.dot(a.astype(BF16), b.astype(BF16), preferred_element_type=F32)


def _dot_nt(a, b):
    return lax.dot_general(a.astype(BF16), b.astype(BF16), (((1,), (1,)), ((), ())),
                           preferred_element_type=F32)


def _dot_tn(a, b):
    return lax.dot_general(a.astype(BF16), b.astype(BF16), (((0,), (0,)), ((), ())),
                           preferred_element_type=F32)


def _mm_body(x_ref, w_ref, o_ref):
    o_ref[...] = _dot(x_ref[...], w_ref[...])


def matmul(x, w, *, tm, tn, name):
    m, k = x.shape
    n = w.shape[1]
    assert m % tm == 0 and n % tn == 0, (m, n, tm, tn)
    return pl.pallas_call(
        _mm_body,
        grid=(m // tm, n // tn),
        in_specs=[pl.BlockSpec((tm, k), lambda i, j: (i, 0)),
                  pl.BlockSpec((k, tn), lambda i, j: (0, j))],
        out_specs=pl.BlockSpec((tm, tn), lambda i, j: (i, j)),
        out_shape=jax.ShapeDtypeStruct((m, n), F32),
        compiler_params=_params("parallel", "arbitrary"),
        name=name,
    )(x, w)


def _row_tile(m, cap):
    t = cap
    while m % t:
        t //= 2
    return t


def project(x2d, w, name):
    m = x2d.shape[0]
    mp = -(-m // 8) * 8
    if mp != m:
        x2d = jnp.pad(x2d, ((0, mp - m), (0, 0)))
    out = matmul(x2d, w, tm=_row_tile(mp, 1024), tn=_row_tile(w.shape[1], 512), name=name)
    return out[:m]


def _cumsum_rows(tri_bf16, x):
    hi = x.astype(BF16)
    r1 = x - hi.astype(F32)
    mid = r1.astype(BF16)
    lo = (r1 - mid.astype(F32)).astype(BF16)
    acc = jnp.dot(tri_bf16, hi, preferred_element_type=F32)
    acc += jnp.dot(tri_bf16, mid, preferred_element_type=F32)
    acc += jnp.dot(tri_bf16, lo, preferred_element_type=F32)
    return acc


def _wkv_body(r_ref, v_ref, kk_ref, lw_ref, k_ref, a_ref, y_ref, s_ref, *, reverse, n_pairs):
    c = WKV_CHUNK
    w2 = 2 * c

    @pl.when(pl.program_id(1) == 0)
    def _():
        s_ref[...] = jnp.zeros_like(s_ref)

    row = lax.broadcasted_iota(jnp.int32, (w2, w2), 0)
    col = lax.broadcasted_iota(jnp.int32, (w2, w2), 1)
    row_a = row < c
    same = row_a == (col < c)
    tt = jnp.where(row_a, row, row - c)
    ss = jnp.where(col < c, col, col - c)
    before = (ss > tt) if reverse else (ss < tt)
    strict = same & before
    incl = same & (before | (ss == tt))
    r64 = lax.broadcasted_iota(jnp.int32, (c, c), 0)
    c64 = lax.broadcasted_iota(jnp.int32, (c, c), 1)
    tri = ((c64 >= r64) if reverse else (c64 <= r64)).astype(BF16)
    lane_a = lax.broadcasted_iota(jnp.int32, (c, w2), 1) < c
    last = 0 if reverse else c - 1

    for p in range(n_pairs):
        sl = slice(p * w2, (p + 1) * w2)
        lw = lw_ref[0, :, sl]
        rr = r_ref[0, :, sl]
        vv = v_ref[0, :, sl]
        kk = kk_ref[0, :, sl]
        kd = k_ref[0, :, sl]
        ad = a_ref[0, :, sl]

        big_l = _cumsum_rows(tri, lw)
        l_tot = big_l[last:last + 1, :]
        e_neg = jnp.exp(-big_l)
        bb = kk * ad
        q_t = kk * jnp.exp(big_l - lw)
        r_t = rr * jnp.exp(big_l)
        k_t = kd * e_neg
        b_t = bb * e_neg
        e_rest = jnp.exp(l_tot - big_l)
        k_p = kd * e_rest
        b_p = bb * e_rest

        zero = jnp.zeros_like(q_t)
        lhs4 = jnp.concatenate([jnp.where(lane_a, q_t, zero), jnp.where(lane_a, zero, q_t),
                                jnp.where(lane_a, r_t, zero), jnp.where(lane_a, zero, r_t)], axis=0)
        res1 = _dot_nt(lhs4, jnp.concatenate([b_t, k_t], axis=0))
        res2 = _dot_nt(lhs4, jnp.concatenate([k_t, b_t], axis=0))
        q1, r1 = res1[:w2], res1[w2:]
        q2, r2 = res2[:w2], res2[w2:]
        zero2 = jnp.zeros_like(q1)
        n_bd = jnp.where(strict, jnp.where(row_a, q1, q2), zero2)
        m_bd = jnp.where(strict, jnp.where(row_a, q2, q1), zero2)
        nr_bd = jnp.where(incl, jnp.where(row_a, r1, r2), zero2)
        mr_bd = jnp.where(incl, jnp.where(row_a, r2, r1), zero2)

        s_bd = s_ref[p]
        qrs = _dot_nt(lhs4, s_bd)
        v_bs = jnp.concatenate([jnp.where(lane_a, vv, zero), jnp.where(lane_a, zero, vv)], axis=0)
        mv = _dot(jnp.concatenate([m_bd, mr_bd], axis=0), v_bs)

        x = qrs[:w2] + mv[:w2]
        pows = [n_bd]
        span = 2
        while span < c:
            pows.append(_dot(pows[-1], pows[-1]))
            span *= 2
        for pk in reversed(pows[1:]):
            x = x + _dot(pk, x)
        u_bs = x - _dot(n_bd, x)

        y_bs = qrs[w2:] + mv[w2:] - _dot(nr_bd, u_bs)
        y_ref[0, :, sl] = y_bs[:c] + y_bs[c:]

        u_pair = u_bs[:c] + u_bs[c:]
        upd = _dot_tn(jnp.concatenate([vv, -u_pair], axis=0), jnp.concatenate([k_p, b_p], axis=0))
        s_ref[p] = s_bd * jnp.exp(l_tot) + jnp.where(same, upd, zero2)


def wkv_scan(r, v, kk, lw, kd, ad, *, n_ctx_chunks, reverse, name):
    b, t, d = r.shape
    c = WKV_CHUNK
    nc = t // c
    lc = n_ctx_chunks
    n_pairs = d // (2 * RWKV_HEAD_DIM)

    if reverse:
        def pos(s):
            return jnp.where(s < lc, lc - 1 - s, nc - 1 - s + lc)
    else:
        def pos(s):
            return s
    spec = pl.BlockSpec((1, c, d), lambda i, s: (i, pos(s), 0))
    return pl.pallas_call(
        functools.partial(_wkv_body, reverse=reverse, n_pairs=n_pairs),
        grid=(b, nc),
        in_specs=[spec] * 6,
        out_specs=spec,
        out_shape=jax.ShapeDtypeStruct((b, t, d), F32),
        scratch_shapes=[pltpu.VMEM((n_pairs, 2 * c, 2 * c), F32)],
        compiler_params=_params("parallel", "arbitrary"),
        name=name,
    )(r, v, kk, lw, kd, ad)


def _gelu(x):
    return 0.5 * x * (1.0 + lax.erf(x * (2.0 ** -0.5)))


def _gmlp_body(u_ref, v_ref, g_ref, b_ref, ws_ref, bs_ref, o_ref, *, groups):
    u = _gelu(u_ref[0])
    v = _gelu(v_ref[0])
    mu = jnp.mean(v, axis=-1, keepdims=True)
    var = jnp.mean(jnp.square(v - mu), axis=-1, keepdims=True)
    vn = (v - mu) * lax.rsqrt(var + GMLP_LN_EPS) * g_ref[...] + b_ref[...]
    gd = vn.shape[1] // groups
    for g in range(groups):
        sl = slice(g * gd, (g + 1) * gd)
        mixed = _dot(ws_ref[g], vn[:, sl]) + bs_ref[g]
        o_ref[0, :, sl] = u[:, sl] * mixed


def gmlp(y, u_block, v_block, ln_g, ln_b, ws, bs, name):
    b, t, _ = y.shape
    groups, chunk, _ = ws.shape
    dg = ln_g.shape[0]
    gd = dg // groups
    bs_b = jnp.broadcast_to(bs[:, :, None], (groups, chunk, gd))
    return pl.pallas_call(
        functools.partial(_gmlp_body, groups=groups),
        grid=(b, t // chunk),
        in_specs=[pl.BlockSpec((1, chunk, dg), lambda i, n: (i, n, u_block)),
                  pl.BlockSpec((1, chunk, dg), lambda i, n: (i, n, v_block)),
                  pl.BlockSpec((1, dg), lambda i, n: (0, 0)),
                  pl.BlockSpec((1, dg), lambda i, n: (0, 0)),
                  pl.BlockSpec((groups, chunk, chunk), lambda i, n: (0, 0, 0)),
                  pl.BlockSpec((groups, chunk, gd), lambda i, n: (0, 0, 0))],
        out_specs=pl.BlockSpec((1, chunk, dg), lambda i, n: (i, n, 0)),
        out_shape=jax.ShapeDtypeStruct((b, t, dg), F32),
        compiler_params=_params("parallel", "parallel"),
        name=name,
    )(y, y, ln_g[None, :], ln_b[None, :], ws, bs_b)


def _attn_body(q_ref, k_ref, v_ref, o_ref, *, group):
    k = k_ref[0].astype(BF16)
    v = v_ref[0].astype(BF16)
    hd = k.shape[1]
    for g in range(group):
        sl = slice(g * hd, (g + 1) * hd)
        s = _dot_nt(q_ref[0, :, sl], k)
        p = jnp.exp(s - jnp.max(s, axis=-1, keepdims=True))
        denom = jnp.sum(p, axis=-1, keepdims=True)
        o_ref[0, :, sl] = jnp.dot(p.astype(BF16), v, preferred_element_type=F32) / denom


def attention(q, k, v, *, n_kv, name):
    b, tq, dq = q.shape
    tk = k.shape[1]
    hd = k.shape[2] // n_kv
    group = dq // hd // n_kv
    bq = _row_tile(tq, 256)
    return pl.pallas_call(
        functools.partial(_attn_body, group=group),
        grid=(b, n_kv, tq // bq),
        in_specs=[pl.BlockSpec((1, bq, group * hd), lambda i, h, j: (i, j, h)),
                  pl.BlockSpec((1, tk, hd), lambda i, h, j: (i, 0, h)),
                  pl.BlockSpec((1, tk, hd), lambda i, h, j: (i, 0, h))],
        out_specs=pl.BlockSpec((1, bq, group * hd), lambda i, h, j: (i, j, h)),
        out_shape=jax.ShapeDtypeStruct((b, tq, dq), F32),
        compiler_params=_params("parallel", "parallel", "arbitrary"),
        name=name,
    )(q, k, v)


def _router_body(h_ref, w_ref, b_ref, idx_ref, wt_ref):
    logits = jnp.dot(h_ref[...], w_ref[...], precision=lax.Precision.HIGHEST,
                     preferred_element_type=F32) + b_ref[...]
    n_e = logits.shape[1]
    lane = lax.broadcasted_iota(jnp.int32, logits.shape, 1)
    vals, ids = [], []
    cur = logits
    for _ in range(TOP_K):
        m = jnp.max(cur, axis=1, keepdims=True)
        sel = jnp.min(jnp.where(cur == m, lane, n_e), axis=1, keepdims=True)
        vals.append(m)
        ids.append(sel)
        cur = jnp.where(lane == sel, -jnp.inf, cur)
    e = jnp.exp(jnp.concatenate(vals, axis=1) - vals[0])
    idx_ref[...] = jnp.concatenate(ids, axis=1)
    wt_ref[...] = e / jnp.sum(e, axis=1, keepdims=True)


def router(h, rw, rb, name):
    t, d = h.shape
    n_e = rw.shape[1]
    tm = _row_tile(t, 512)
    return pl.pallas_call(
        _router_body,
        grid=(t // tm,),
        in_specs=[pl.BlockSpec((tm, d), lambda i: (i, 0)),
                  pl.BlockSpec((d, n_e), lambda i: (0, 0)),
                  pl.BlockSpec((1, n_e), lambda i: (0, 0))],
        out_specs=[pl.BlockSpec((tm, TOP_K), lambda i: (i, 0)),
                   pl.BlockSpec((tm, TOP_K), lambda i: (i, 0))],
        out_shape=[jax.ShapeDtypeStruct((t, TOP_K), jnp.int32),
                   jax.ShapeDtypeStruct((t, TOP_K), F32)],
        compiler_params=_params("parallel"),
        name=name,
    )(h, rw, rb[None, :])


def _ffn_body(te_ref, nv_ref, x_ref, wg_ref, wl_ref, bg_ref, bl_ref, w2_ref, b2_ref, o_ref):
    i = pl.program_id(0)
    c = pl.program_id(1)
    valid = i < nv_ref[0]

    @pl.when(valid)
    def _():
        x = x_ref[...]
        glu = _dot(x, wg_ref[0]) + bg_ref[0]
        lin = _dot(x, wl_ref[0]) + bl_ref[0]
        glu = jnp.minimum(glu, SWIGLU_LIMIT)
        lin = jnp.clip(lin, -SWIGLU_LIMIT, SWIGLU_LIMIT)
        act = glu * jax.nn.sigmoid(SWIGLU_ALPHA * glu) * (lin + 1.0)
        part = _dot(act, w2_ref[0])

        @pl.when(c == 0)
        def _():
            o_ref[...] = part + b2_ref[0]

        @pl.when(c > 0)
        def _():
            o_ref[...] += part

    @pl.when(jnp.logical_not(valid) & (c == 0))
    def _():
        o_ref[...] = jnp.zeros_like(o_ref)


def moe_ffn(xs, tile_expert, n_valid, w1, b1, w2, b2, *, tm, fc, name):
    np_rows, d = xs.shape
    n_e, _, f2 = w1.shape
    f = f2 // 2
    nfc = f // fc
    n_tiles = np_rows // tm
    grid_spec = pltpu.PrefetchScalarGridSpec(
        num_scalar_prefetch=2,
        grid=(n_tiles, nfc),
        in_specs=[pl.BlockSpec((tm, d), lambda i, c, te, nv: (i, 0)),
                  pl.BlockSpec((1, d, fc), lambda i, c, te, nv: (te[i], 0, c)),
                  pl.BlockSpec((1, d, fc), lambda i, c, te, nv: (te[i], 0, nfc + c)),
                  pl.BlockSpec((1, 1, fc), lambda i, c, te, nv: (te[i], 0, c)),
                  pl.BlockSpec((1, 1, fc), lambda i, c, te, nv: (te[i], 0, nfc + c)),
                  pl.BlockSpec((1, fc, d), lambda i, c, te, nv: (te[i], c, 0)),
                  pl.BlockSpec((1, 1, d), lambda i, c, te, nv: (te[i], 0, 0))],
        out_specs=pl.BlockSpec((tm, d), lambda i, c, te, nv: (i, 0)),
    )
    return pl.pallas_call(
        _ffn_body,
        grid_spec=grid_spec,
        out_shape=jax.ShapeDtypeStruct((np_rows, d), F32),
        compiler_params=_params("arbitrary", "arbitrary"),
        name=name,
    )(tile_expert, n_valid, xs, w1, w1, b1[:, None, :], b1[:, None, :], w2, b2[:, None, :])


def moe(tokens, rw, rb, w1, b1, w2, b2, tag):
    t, d = tokens.shape
    n_e = rw.shape[1]
    tm = 1024 if t * TOP_K >= 1024 * n_e else 128
    fc = 256
    idx, wts = router(tokens, rw, rb, "router" + tag)

    e_flat = idx.reshape(-1)
    n_pairs = t * TOP_K
    order = jnp.argsort(e_flat, stable=True)
    e_sorted = e_flat[order]
    counts = jnp.zeros((n_e,), jnp.int32).at[e_flat].add(1)
    padded = (counts + tm - 1) // tm * tm
    starts = jnp.cumsum(counts) - counts
    starts_p = jnp.cumsum(padded) - padded
    dest_sorted = starts_p[e_sorted] + jnp.arange(n_pairs, dtype=jnp.int32) - starts[e_sorted]
    n_tiles = -(-n_pairs // tm) + n_e
    np_rows = n_tiles * tm
    src_tok = jnp.zeros((np_rows,), jnp.int32).at[dest_sorted].set(order // TOP_K)
    dest = jnp.zeros((n_pairs,), jnp.int32).at[order].set(dest_sorted).reshape(t, TOP_K)
    ends_p = jnp.cumsum(padded)
    n_valid = (ends_p[-1] // tm).astype(jnp.int32)
    tile_start = jnp.arange(n_tiles, dtype=jnp.int32) * tm
    tile_expert = jnp.searchsorted(ends_p, tile_start, side="right").astype(jnp.int32)
    last_e = jnp.searchsorted(ends_p, ends_p[-1] - 1, side="right").astype(jnp.int32)
    tile_expert = jnp.minimum(tile_expert, last_e)

    xs = tokens.astype(BF16)[src_tok]
    ys = moe_ffn(xs, tile_expert, n_valid[None], w1, b1, w2, b2, tm=tm, fc=fc, name="moe_ffn" + tag)
    return jnp.sum(ys[dest] * wts[:, :, None], axis=1)


def _rmsnorm(x, g):
    return x * lax.rsqrt(jnp.mean(x * x, axis=-1, keepdims=True) + NORM_EPS) * g


def _adaln(cvec, w_mod, b_mod, name):
    m = project(jax.nn.silu(cvec), w_mod, name) + b_mod
    return jnp.split(m, N_MOD, axis=-1)


def _centred_shift(y, mu):
    pad = jnp.pad(y, ((0, 0), (1, 1), (0, 0)))
    nb = 0.5 * (pad[:, :-2] + pad[:, 2:])
    return y + mu * (nb - y)


def _even_layout(d_rwkv, lora_w, lora_a, lora_g, d_gmlp):
    pad = lambda n: -(-n // LANES) * LANES
    d_shift = 3 * d_rwkv + lora_w + lora_a + lora_g
    src = {"rkv": (0, 3 * d_rwkv), "xw": (3 * d_rwkv, lora_w), "xa": (3 * d_rwkv + lora_w, lora_a),
           "xg": (3 * d_rwkv + lora_w + lora_a, lora_g), "gmlp": (d_shift, 2 * d_gmlp)}
    order = ["rkv", "gmlp", "xw", "xa", "xg"]
    dst, off = {}, 0
    for name in order:
        dst[name] = off
        off += pad(src[name][1])
    return src, dst, order, off


def _regroup_cols(w, src, dst, order, total):
    parts = []
    for name in order:
        s, n = src[name]
        blk = w[..., s:s + n]
        padn = -(-n // LANES) * LANES - n
        if padn:
            blk = jnp.pad(blk, [(0, 0)] * (w.ndim - 1) + [(0, padn)])
        parts.append(blk)
    return jnp.concatenate(parts, axis=-1)


def _even_mixer(nx, nc, p, need_ctx, tag):
    (w_in, mu, w0, w_up, a0, a_up, g_up, k_k, k_a, r_k, lnx_g, lnx_b, v_ln_g, v_ln_b, ws, bs, w_out) = p
    b, s, d = nx.shape
    l = nc.shape[1]
    d_rwkv = g_up.shape[1]
    heads = d_rwkv // RWKV_HEAD_DIM
    lora_w, lora_a, lora_g = w_up.shape[1], a_up.shape[1], g_up.shape[0]
    d_gmlp = v_ln_g.shape[0]
    src, dst, order, total = _even_layout(d_rwkv, lora_w, lora_a, lora_g, d_gmlp)
    w_in_r = _regroup_cols(w_in, src, dst, order, total)
    d_shift = 3 * d_rwkv + lora_w + lora_a + lora_g
    mu_r = _regroup_cols(jnp.concatenate([mu, jnp.zeros((2 * d_gmlp,), F32)]), src, dst, order, total)

    yx = project(nx.reshape(b * s, d), w_in_r, "even_in_x" + tag).reshape(b, s, total)
    yc = project(nc.reshape(b * l, d), w_in_r, "even_in_c" + tag).reshape(b, l, total)

    def prepare(y):
        ys = _centred_shift(y, mu_r)
        r = ys[..., 0:d_rwkv]
        k = ys[..., d_rwkv:2 * d_rwkv]
        v = ys[..., 2 * d_rwkv:3 * d_rwkv]
        xw = ys[..., dst["xw"]:dst["xw"] + lora_w]
        xa = ys[..., dst["xa"]:dst["xa"] + lora_a]
        xg = ys[..., dst["xg"]:dst["xg"] + lora_g]
        bt = y.shape[:2]
        hd = lambda t: t.reshape(*bt, heads, RWKV_HEAD_DIM)
        kk = hd(k * k_k)
        kk = (kk * lax.rsqrt(jnp.sum(kk * kk, axis=-1, keepdims=True) + 1e-12)).reshape(*bt, d_rwkv)
        g = jnp.dot(jax.nn.sigmoid(xg), g_up)
        lws, kds, ads = [], [], []
        for dd in range(2):
            w_log = -jax.nn.softplus(-(w0[dd] + jnp.dot(jnp.tanh(xw), w_up[dd]))) - 0.5
            a = jax.nn.sigmoid(a0[dd] + jnp.dot(xa, a_up[dd]))
            lws.append(-jnp.exp(w_log))
            kds.append(k * (1.0 + (a - 1.0) * k_a))
            ads.append(a)
        return r, v, kk, g, lws, kds, ads

    px = prepare(yx)
    pc = prepare(yc)
    cat = lambda a, bb: jnp.concatenate([a, bb], axis=1)
    r_all, v_all, kk_all, g_all = (cat(pc[i], px[i]) for i in range(4))
    y_sum = None
    for dd in range(2):
        y_d = wkv_scan(r_all, v_all, kk_all, cat(pc[4][dd], px[4][dd]), cat(pc[5][dd], px[5][dd]),
                       cat(pc[6][dd], px[6][dd]), n_ctx_chunks=l // WKV_CHUNK, reverse=dd == 1,
                       name="wkv%d%s" % (dd, tag))
        y_sum = y_d if y_sum is None else y_sum + y_d

    t_all = l + s
    hd = lambda t: t.reshape(b, t_all, heads, RWKV_HEAD_DIM)
    k0 = cat(pc[5][0], px[5][0])
    k1 = cat(pc[5][1], px[5][1])
    bonus = jnp.sum(hd(r_all) * hd(k0 + k1) * r_k, axis=-1, keepdims=True) * hd(v_all)
    wk = hd(y_sum)
    mean = jnp.mean(wk, axis=-1, keepdims=True)
    var = jnp.mean(jnp.square(wk - mean), axis=-1, keepdims=True)
    gn = ((wk - mean) * lax.rsqrt(var + RWKV_GN_EPS)).reshape(b, t_all, d_rwkv) * lnx_g + lnx_b
    rw_out = (gn + bonus.reshape(b, t_all, d_rwkv)) * g_all

    ub, vb = dst["gmlp"] // d_gmlp, dst["gmlp"] // d_gmlp + 1
    assert dst["gmlp"] % d_gmlp == 0
    gx = gmlp(yx, ub, vb, v_ln_g, v_ln_b, ws, bs, "gmlp_x" + tag)
    mix_x = project(jnp.concatenate([rw_out[:, l:], gx], axis=-1).reshape(b * s, -1), w_out,
                    "even_out_x" + tag).reshape(b, s, d)
    if not need_ctx:
        return mix_x, None
    gc = gmlp(yc, ub, vb, v_ln_g, v_ln_b, ws, bs, "gmlp_c" + tag)
    mix_c = project(jnp.concatenate([rw_out[:, :l], gc], axis=-1).reshape(b * l, -1), w_out,
                    "even_out_c" + tag).reshape(b, l, d)
    return mix_x, mix_c


def _rope_tables(rows):
    row = jnp.repeat(jnp.arange(rows, dtype=F32), GRID_W)
    col = jnp.tile(jnp.arange(GRID_W, dtype=F32), rows)
    n_freq = ATT_HEAD_DIM // 4
    inv = jnp.power(jnp.float32(ROPE_THETA), -jnp.arange(n_freq, dtype=F32) / n_freq)
    ang_r = (row[:, None] * inv)[:, None, :]
    ang_c = (col[:, None] * inv)[:, None, :]
    return jnp.cos(ang_r), jnp.sin(ang_r), jnp.cos(ang_c), jnp.sin(ang_c)


def _rotate(x, cos, sin):
    x1, x2 = jnp.split(x, 2, axis=-1)
    return jnp.concatenate([x1 * cos - x2 * sin, x1 * sin + x2 * cos], axis=-1)


def _axial_rope(x, rope):
    cos_r, sin_r, cos_c, sin_c = rope
    xr, xc = jnp.split(x, 2, axis=-1)
    return jnp.concatenate([_rotate(xr, cos_r, sin_r), _rotate(xc, cos_c, sin_c)], axis=-1)


def _odd_mixer(nx, nc, p, rope, need_ctx, tag):
    w_in, q_g, k_g, w_out = p
    b, s, d = nx.shape
    l = nc.shape[1]
    hd = ATT_HEAD_DIM
    n_q = w_out.shape[0] // hd
    n_kv = (w_in.shape[1] // hd - n_q) // 2

    def split(h, t, name):
        y = project(h.reshape(b * t, d), w_in, name).reshape(b, t, -1)
        q = _rmsnorm(y[..., :n_q * hd].reshape(b, t, n_q, hd), q_g)
        k = _rmsnorm(y[..., n_q * hd:(n_q + n_kv) * hd].reshape(b, t, n_kv, hd), k_g)
        return q, k, y[..., (n_q + n_kv) * hd:]

    qx, kx, vx = split(nx, s, "odd_in_x" + tag)
    qc, kc, vc = split(nc, l, "odd_in_c" + tag)
    qx = _axial_rope(qx, rope)
    kx = _axial_rope(kx, rope)
    scale = hd ** -0.5
    keys = jnp.concatenate([kc, kx], axis=1).reshape(b, l + s, n_kv * hd)
    vals = jnp.concatenate([vc, vx], axis=1)
    ox = attention((qx * scale).reshape(b, s, n_q * hd), keys, vals, n_kv=n_kv, name="attn_x" + tag)
    mix_x = project(ox.reshape(b * s, -1), w_out, "odd_out_x" + tag).reshape(b, s, d)
    if not need_ctx:
        return mix_x, None
    oc = attention((qc * scale).reshape(b, l, n_q * hd), kc.reshape(b, l, n_kv * hd), vc, n_kv=n_kv,
                   name="attn_c" + tag)
    mix_c = project(oc.reshape(b * l, -1), w_out, "odd_out_c" + tag).reshape(b, l, d)
    return mix_x, mix_c


def kernel(x, c, ctx, c_ctx, norm1_g, norm2_g, w_mod, b_mod, e_w_in, e_mu, e_w0, e_w_up, e_a0, e_a_up, e_g_up, e_k_k, e_k_a, e_r_k, e_lnx_g, e_lnx_b, e_v_ln_g, e_v_ln_b, e_ws, e_bs, e_w_out, o_w_in, o_q_g, o_k_g, o_w_out, router_w, router_b, moe_w1, moe_b1, moe_w2, moe_b2, final_g):
    b, s, d = x.shape
    l = ctx.shape[1]
    depth = norm1_g.shape[0]
    rope = _rope_tables(s // GRID_W)
    h_lat, h_ctx = x, ctx
    for i in range(depth):
        tag = "_l%d" % i
        need_ctx = i < depth - 1
        j = i // 2
        mods = _adaln(jnp.concatenate([c, c_ctx[None, :]], axis=0), w_mod[i], b_mod[i], "adaln" + tag)
        m_lat = [m[:b, None, :] for m in mods]
        m_ctx = [m[b:, None, :] for m in mods]
        nx = _rmsnorm(h_lat, norm1_g[i]) * (1.0 + m_lat[1]) + m_lat[0]
        nc = _rmsnorm(h_ctx, norm1_g[i]) * (1.0 + m_ctx[1]) + m_ctx[0]
        if i % 2 == 0:
            p = (e_w_in[j], e_mu[j], e_w0[j], e_w_up[j], e_a0[j], e_a_up[j], e_g_up[j], e_k_k[j], e_k_a[j],
                 e_r_k[j], e_lnx_g[j], e_lnx_b[j], e_v_ln_g[j], e_v_ln_b[j], e_ws[j], e_bs[j], e_w_out[j])
            mx, mc = _even_mixer(nx, nc, p, need_ctx, tag)
        else:
            mx, mc = _odd_mixer(nx, nc, (o_w_in[j], o_q_g[j], o_k_g[j], o_w_out[j]), rope, need_ctx, tag)
        h_lat = h_lat + m_lat[2] * mx
        nx = _rmsnorm(h_lat, norm2_g[i]) * (1.0 + m_lat[4]) + m_lat[3]
        if need_ctx:
            h_ctx = h_ctx + m_ctx[2] * mc
            nc = _rmsnorm(h_ctx, norm2_g[i]) * (1.0 + m_ctx[4]) + m_ctx[3]
            tokens = jnp.concatenate([nx.reshape(b * s, d), nc.reshape(b * l, d)], axis=0)
        else:
            tokens = nx.reshape(b * s, d)
        f = moe(tokens, router_w[i], router_b[i], moe_w1[i], moe_b1[i], moe_w2[i], moe_b2[i], tag)
        h_lat = h_lat + m_lat[5] * f[:b * s].reshape(b, s, d)
        if need_ctx:
            h_ctx = h_ctx + m_ctx[5] * f[b * s:].reshape(b, l, d)
    return _rmsnorm(h_lat, final_g)
```

```python
import functools

import jax
import jax.numpy as jnp
from jax import lax
from jax.experimental import pallas as pl
from jax.experimental.pallas import tpu as pltpu

F32 = jnp.float32
BF16 = jnp.bfloat16

NORM_EPS = 1e-6
N_MOD = 6
RWKV_HEAD_DIM = 64
RWKV_GN_EPS = 64e-5
GMLP_LN_EPS = 1e-5
ATT_HEAD_DIM = 128
GRID_W = 64
ROPE_THETA = 10000.0
TOP_K = 4
SWIGLU_ALPHA = 1.702
SWIGLU_LIMIT = 7.0

LANES = 128
VMEM_LIMIT_BYTES = 56 * 1024 * 1024

WKV_CHUNK = 64


def _params(*sem):
    return pltpu.CompilerParams(dimension_semantics=sem, vmem_limit_bytes=VMEM_LIMIT_BYTES)


def _dot(a, b):
    return jnp.dot(a.astype(BF16), b.astype(BF16), preferred_element_type=F32)


def _dot_nt(a, b):
    return lax.dot_general(a.astype(BF16), b.astype(BF16), (((1,), (1,)), ((), ())),
                           preferred_element_type=F32)


def _dot_tn(a, b):
    return lax.dot_general(a.astype(BF16), b.astype(BF16), (((0,), (0,)), ((), ())),
                           preferred_element_type=F32)


def _mm_body(x_ref, w_ref, o_ref):
    o_ref[...] = _dot(x_ref[...], w_ref[...])


def matmul(x, w, *, tm, tn, name):
    m, k = x.shape
    n = w.shape[1]
    assert m % tm == 0 and n % tn == 0, (m, n, tm, tn)
    return pl.pallas_call(
        _mm_body,
        grid=(m // tm, n // tn),
        in_specs=[pl.BlockSpec((tm, k), lambda i, j: (i, 0)),
                  pl.BlockSpec((k, tn), lambda i, j: (0, j))],
        out_specs=pl.BlockSpec((tm, tn), lambda i, j: (i, j)),
        out_shape=jax.ShapeDtypeStruct((m, n), F32),
        compiler_params=_params("parallel", "arbitrary"),
        name=name,
    )(x, w)


def _row_tile(m, cap):
    t = cap
    while m % t:
        t //= 2
    return t


def project(x2d, w, name):
    m = x2d.shape[0]
    mp = -(-m // 8) * 8
    if mp != m:
        x2d = jnp.pad(x2d, ((0, mp - m), (0, 0)))
    out = matmul(x2d, w, tm=_row_tile(mp, 1024), tn=_row_tile(w.shape[1], 512), name=name)
    return out[:m]


def _cumsum_rows(tri_bf16, x):
    hi = x.astype(BF16)
    r1 = x - hi.astype(F32)
    mid = r1.astype(BF16)
    lo = (r1 - mid.astype(F32)).astype(BF16)
    parts = jnp.concatenate([hi, mid, lo], axis=1)
    acc = jnp.dot(tri_bf16, parts, preferred_element_type=F32)
    n = x.shape[1]
    return acc[:, :n] + acc[:, n:2 * n] + acc[:, 2 * n:]


def _bdot(a, b):
    return jnp.dot(a, b, preferred_element_type=F32)


def _wkv_body(r_ref, v_ref, kk_ref, lw_ref, k_ref, a_ref, y_ref, s_ref, *, reverse, n_pairs):
    c = WKV_CHUNK
    w2 = 2 * c

    @pl.when(pl.program_id(1) == 0)
    def _():
        s_ref[...] = jnp.zeros_like(s_ref)

    row = lax.broadcasted_iota(jnp.int32, (w2, w2), 0)
    col = lax.broadcasted_iota(jnp.int32, (w2, w2), 1)
    row_a = row < c
    same = row_a == (col < c)
    tt = jnp.where(row_a, row, row - c)
    ss = jnp.where(col < c, col, col - c)
    before = (ss > tt) if reverse else (ss < tt)
    strict = same & before
    incl = same & (before | (ss == tt))
    r64 = lax.broadcasted_iota(jnp.int32, (c, c), 0)
    c64 = lax.broadcasted_iota(jnp.int32, (c, c), 1)
    tri = ((c64 >= r64) if reverse else (c64 <= r64)).astype(BF16)
    lane_a = lax.broadcasted_iota(jnp.int32, (c, w2), 1) < c
    last = 0 if reverse else c - 1
    pairs = range(n_pairs)
    sls = [slice(p * w2, (p + 1) * w2) for p in pairs]
    zero = jnp.zeros((c, w2), F32)
    zero2 = jnp.zeros((w2, w2), F32)

    def split_heads(x):
        return jnp.concatenate([jnp.where(lane_a, x, zero), jnp.where(lane_a, zero, x)], axis=0)

    lw = [lw_ref[0, :, sl] for sl in sls]
    big_l = [_cumsum_rows(tri, lw[p]) for p in pairs]
    l_tot = [big_l[p][last:last + 1, :] for p in pairs]

    kk = [kk_ref[0, :, sl] for sl in sls]
    kd = [k_ref[0, :, sl] for sl in sls]
    bb = [kk[p] * a_ref[0, :, sls[p]] for p in pairs]
    vv = [v_ref[0, :, sl] for sl in sls]
    e_neg = [jnp.exp(-big_l[p]) for p in pairs]
    q_t = [kk[p] * jnp.exp(big_l[p] - lw[p]) for p in pairs]
    r_t = [r_ref[0, :, sls[p]] * jnp.exp(big_l[p]) for p in pairs]
    k_t = [(kd[p] * e_neg[p]).astype(BF16) for p in pairs]
    b_t = [(bb[p] * e_neg[p]).astype(BF16) for p in pairs]
    e_rest = [jnp.exp(l_tot[p] - big_l[p]) for p in pairs]
    kb_p = [jnp.concatenate([kd[p] * e_rest[p], bb[p] * e_rest[p]], axis=0).astype(BF16) for p in pairs]

    lhs4 = [jnp.concatenate([split_heads(q_t[p]), split_heads(r_t[p])], axis=0).astype(BF16) for p in pairs]
    s_bd = [s_ref[p] for p in pairs]
    big = [lax.dot_general(lhs4[p],
                           jnp.concatenate([b_t[p], k_t[p], k_t[p], b_t[p], s_bd[p].astype(BF16)], axis=0),
                           (((1,), (1,)), ((), ())), preferred_element_type=F32) for p in pairs]
    n_bd, m_both, nr_bd = [], [], []
    for p in pairs:
        q1, r1 = big[p][:w2, :w2], big[p][w2:, :w2]
        q2, r2 = big[p][:w2, w2:2 * w2], big[p][w2:, w2:2 * w2]
        n_bd.append(jnp.where(strict, jnp.where(row_a, q1, q2), zero2).astype(BF16))
        nr_bd.append(jnp.where(incl, jnp.where(row_a, r1, r2), zero2).astype(BF16))
        m_both.append(jnp.concatenate([jnp.where(strict, jnp.where(row_a, q2, q1), zero2),
                                       jnp.where(incl, jnp.where(row_a, r2, r1), zero2)], axis=0).astype(BF16))
    mv = [_bdot(m_both[p], split_heads(vv[p]).astype(BF16)) for p in pairs]

    x = [big[p][:w2, 2 * w2:] + mv[p][:w2] for p in pairs]
    pw = n_bd
    span = 1
    while span < c:
        if 2 * span < c:
            both = [_bdot(pw[p], jnp.concatenate([x[p].astype(BF16), pw[p]], axis=1)) for p in pairs]
            pw = [both[p][:, w2:].astype(BF16) for p in pairs]
        else:
            both = [_bdot(pw[p], x[p].astype(BF16)) for p in pairs]
        x = [(x[p] - both[p][:, :w2]) if span == 1 else (x[p] + both[p][:, :w2]) for p in pairs]
        span *= 2
    u_bs = x

    nru = [_bdot(nr_bd[p], u_bs[p].astype(BF16)) for p in pairs]
    for p in pairs:
        y_bs = big[p][w2:, 2 * w2:] + mv[p][w2:] - nru[p]
        y_ref[0, :, sls[p]] = y_bs[:c] + y_bs[c:]

    vu = [jnp.concatenate([vv[p], -(u_bs[p][:c] + u_bs[p][c:])], axis=0).astype(BF16) for p in pairs]
    upd = [lax.dot_general(vu[p], kb_p[p], (((0,), (0,)), ((), ())), preferred_element_type=F32)
           for p in pairs]
    for p in pairs:
        s_ref[p] = s_bd[p] * jnp.exp(l_tot[p]) + jnp.where(same, upd[p], zero2)


def wkv_scan(r, v, kk, lw, kd, ad, *, n_ctx_chunks, reverse, name):
    b, t, d = r.shape
    c = WKV_CHUNK
    nc = t // c
    lc = n_ctx_chunks
    n_pairs = d // (2 * RWKV_HEAD_DIM)

    if reverse:
        def pos(s):
            return jnp.where(s < lc, lc - 1 - s, nc - 1 - s + lc)
    else:
        def pos(s):
            return s
    spec = pl.BlockSpec((1, c, d), lambda i, s: (i, pos(s), 0))
    return pl.pallas_call(
        functools.partial(_wkv_body, reverse=reverse, n_pairs=n_pairs),
        grid=(b, nc),
        in_specs=[spec] * 6,
        out_specs=spec,
        out_shape=jax.ShapeDtypeStruct((b, t, d), F32),
        scratch_shapes=[pltpu.VMEM((n_pairs, 2 * c, 2 * c), F32)],
        compiler_params=_params("parallel", "arbitrary"),
        name=name,
    )(r, v, kk, lw, kd, ad)


def _gelu(x):
    return 0.5 * x * (1.0 + lax.erf(x * (2.0 ** -0.5)))


def _gmlp_body(u_ref, v_ref, g_ref, b_ref, ws_ref, bs_ref, o_ref, *, groups):
    u = _gelu(u_ref[0])
    v = _gelu(v_ref[0])
    mu = jnp.mean(v, axis=-1, keepdims=True)
    var = jnp.mean(jnp.square(v - mu), axis=-1, keepdims=True)
    vn = (v - mu) * lax.rsqrt(var + GMLP_LN_EPS) * g_ref[...] + b_ref[...]
    gd = vn.shape[1] // groups
    for g in range(groups):
        sl = slice(g * gd, (g + 1) * gd)
        mixed = _dot(ws_ref[g], vn[:, sl]) + bs_ref[g]
        o_ref[0, :, sl] = u[:, sl] * mixed


def gmlp(y, u_block, v_block, ln_g, ln_b, ws, bs, name):
    b, t, _ = y.shape
    groups, chunk, _ = ws.shape
    dg = ln_g.shape[0]
    gd = dg // groups
    bs_b = jnp.broadcast_to(bs[:, :, None], (groups, chunk, gd))
    return pl.pallas_call(
        functools.partial(_gmlp_body, groups=groups),
        grid=(b, t // chunk),
        in_specs=[pl.BlockSpec((1, chunk, dg), lambda i, n: (i, n, u_block)),
                  pl.BlockSpec((1, chunk, dg), lambda i, n: (i, n, v_block)),
                  pl.BlockSpec((1, dg), lambda i, n: (0, 0)),
                  pl.BlockSpec((1, dg), lambda i, n: (0, 0)),
                  pl.BlockSpec((groups, chunk, chunk), lambda i, n: (0, 0, 0)),
                  pl.BlockSpec((groups, chunk, gd), lambda i, n: (0, 0, 0))],
        out_specs=pl.BlockSpec((1, chunk, dg), lambda i, n: (i, n, 0)),
        out_shape=jax.ShapeDtypeStruct((b, t, dg), F32),
        compiler_params=_params("parallel", "parallel"),
        name=name,
    )(y, y, ln_g[None, :], ln_b[None, :], ws, bs_b)


def _attn_body(q_ref, k_ref, v_ref, o_ref, *, group):
    k = k_ref[0].astype(BF16)
    v = v_ref[0].astype(BF16)
    hd = k.shape[1]
    for g in range(group):
        sl = slice(g * hd, (g + 1) * hd)
        s = _dot_nt(q_ref[0, :, sl], k)
        p = jnp.exp(s - jnp.max(s, axis=-1, keepdims=True))
        denom = jnp.sum(p, axis=-1, keepdims=True)
        o_ref[0, :, sl] = jnp.dot(p.astype(BF16), v, preferred_element_type=F32) / denom


def attention(q, k, v, *, n_kv, name):
    b, tq, dq = q.shape
    tk = k.shape[1]
    hd = k.shape[2] // n_kv
    group = dq // hd // n_kv
    bq = _row_tile(tq, 256)
    return pl.pallas_call(
        functools.partial(_attn_body, group=group),
        grid=(b, n_kv, tq // bq),
        in_specs=[pl.BlockSpec((1, bq, group * hd), lambda i, h, j: (i, j, h)),
                  pl.BlockSpec((1, tk, hd), lambda i, h, j: (i, 0, h)),
                  pl.BlockSpec((1, tk, hd), lambda i, h, j: (i, 0, h))],
        out_specs=pl.BlockSpec((1, bq, group * hd), lambda i, h, j: (i, j, h)),
        out_shape=jax.ShapeDtypeStruct((b, tq, dq), F32),
        compiler_params=_params("parallel", "parallel", "arbitrary"),
        name=name,
    )(q, k, v)


def _router_body(h_ref, w_ref, b_ref, idx_ref, wt_ref):
    logits = jnp.dot(h_ref[...], w_ref[...], precision=lax.Precision.HIGHEST,
                     preferred_element_type=F32) + b_ref[...]
    n_e = logits.shape[1]
    lane = lax.broadcasted_iota(jnp.int32, logits.shape, 1)
    vals, ids = [], []
    cur = logits
    for _ in range(TOP_K):
        m = jnp.max(cur, axis=1, keepdims=True)
        sel = jnp.min(jnp.where(cur == m, lane, n_e), axis=1, keepdims=True)
        vals.append(m)
        ids.append(sel)
        cur = jnp.where(lane == sel, -jnp.inf, cur)
    e = jnp.exp(jnp.concatenate(vals, axis=1) - vals[0])
    idx_ref[...] = jnp.concatenate(ids, axis=1)
    wt_ref[...] = e / jnp.sum(e, axis=1, keepdims=True)


def router(h, rw, rb, name):
    t, d = h.shape
    n_e = rw.shape[1]
    tm = _row_tile(t, 512)
    return pl.pallas_call(
        _router_body,
        grid=(t // tm,),
        in_specs=[pl.BlockSpec((tm, d), lambda i: (i, 0)),
                  pl.BlockSpec((d, n_e), lambda i: (0, 0)),
                  pl.BlockSpec((1, n_e), lambda i: (0, 0))],
        out_specs=[pl.BlockSpec((tm, TOP_K), lambda i: (i, 0)),
                   pl.BlockSpec((tm, TOP_K), lambda i: (i, 0))],
        out_shape=[jax.ShapeDtypeStruct((t, TOP_K), jnp.int32),
                   jax.ShapeDtypeStruct((t, TOP_K), F32)],
        compiler_params=_params("parallel"),
        name=name,
    )(h, rw, rb[None, :])


def _ffn_body(te_ref, nv_ref, x_ref, wg_ref, wl_ref, bg_ref, bl_ref, w2_ref, b2_ref, o_ref):
    i = pl.program_id(0)
    c = pl.program_id(1)
    valid = i < nv_ref[0]

    @pl.when(valid)
    def _():
        x = x_ref[...]
        glu = _dot(x, wg_ref[0]) + bg_ref[0]
        lin = _dot(x, wl_ref[0]) + bl_ref[0]
        glu = jnp.minimum(glu, SWIGLU_LIMIT)
        lin = jnp.clip(lin, -SWIGLU_LIMIT, SWIGLU_LIMIT)
        act = glu * jax.nn.sigmoid(SWIGLU_ALPHA * glu) * (lin + 1.0)
        part = _dot(act, w2_ref[0])

        @pl.when(c == 0)
        def _():
            o_ref[...] = part + b2_ref[0]

        @pl.when(c > 0)
        def _():
            o_ref[...] += part

    @pl.when(jnp.logical_not(valid) & (c == 0))
    def _():
        o_ref[...] = jnp.zeros_like(o_ref)


def moe_ffn(xs, tile_expert, n_valid, w1, b1, w2, b2, *, layer, tm, fc, name):
    np_rows, d = xs.shape
    f = w2.shape[2]
    nfc = f // fc
    n_tiles = np_rows // tm
    grid_spec = pltpu.PrefetchScalarGridSpec(
        num_scalar_prefetch=2,
        grid=(n_tiles, nfc),
        in_specs=[pl.BlockSpec((tm, d), lambda i, c, te, nv: (i, 0)),
                  pl.BlockSpec((None, 1, d, fc), lambda i, c, te, nv: (layer, te[i], 0, c)),
                  pl.BlockSpec((None, 1, d, fc), lambda i, c, te, nv: (layer, te[i], 0, nfc + c)),
                  pl.BlockSpec((1, 1, fc), lambda i, c, te, nv: (te[i], 0, c)),
                  pl.BlockSpec((1, 1, fc), lambda i, c, te, nv: (te[i], 0, nfc + c)),
                  pl.BlockSpec((None, 1, fc, d), lambda i, c, te, nv: (layer, te[i], c, 0)),
                  pl.BlockSpec((1, 1, d), lambda i, c, te, nv: (te[i], 0, 0))],
        out_specs=pl.BlockSpec((tm, d), lambda i, c, te, nv: (i, 0)),
    )
    return pl.pallas_call(
        _ffn_body,
        grid_spec=grid_spec,
        out_shape=jax.ShapeDtypeStruct((np_rows, d), F32),
        compiler_params=_params("arbitrary", "arbitrary"),
        name=name,
    )(tile_expert, n_valid, xs, w1, w1, b1, b1, w2, b2)


def moe(tokens, rw, rb, w1, b1, w2, b2, layer, tag):
    t, d = tokens.shape
    n_e = rw.shape[1]
    tm = 1024 if t * TOP_K >= 1024 * n_e else 128
    fc = 256
    idx, wts = router(tokens, rw, rb, "router" + tag)

    e_flat = idx.reshape(-1)
    n_pairs = t * TOP_K
    order = jnp.argsort(e_flat, stable=True).astype(jnp.int32)
    rank = jnp.argsort(order).astype(jnp.int32)
    bounds = jnp.searchsorted(e_flat[order], jnp.arange(n_e + 1, dtype=jnp.int32), side="left")
    bounds = bounds.astype(jnp.int32)
    starts, counts = bounds[:-1], bounds[1:] - bounds[:-1]
    padded = (counts + tm - 1) // tm * tm
    ends_p = jnp.cumsum(padded)
    starts_p = ends_p - padded
    dest = (starts_p[e_flat] + rank - starts[e_flat]).reshape(t, TOP_K)
    n_tiles = -(-n_pairs // tm) + n_e
    np_rows = n_tiles * tm
    n_valid = (ends_p[-1] // tm).astype(jnp.int32)
    tile_start = jnp.arange(n_tiles, dtype=jnp.int32) * tm
    tile_expert = jnp.searchsorted(ends_p, tile_start, side="right").astype(jnp.int32)
    last_e = jnp.searchsorted(ends_p, ends_p[-1] - 1, side="right").astype(jnp.int32)
    tile_expert = jnp.minimum(tile_expert, last_e)
    q = jnp.arange(np_rows, dtype=jnp.int32)
    row_e = tile_expert[q // tm]
    local = q - starts_p[row_e]
    row_ok = (local < counts[row_e]) & (q < ends_p[-1])
    src_tok = jnp.where(row_ok, order[jnp.clip(starts[row_e] + local, 0, n_pairs - 1)] // TOP_K, 0)

    xs = tokens.astype(BF16)[src_tok]
    ys = moe_ffn(xs, tile_expert, n_valid[None], w1, b1[:, None, :], w2, b2[:, None, :], layer=layer,
                 tm=tm, fc=fc, name="moe_ffn" + tag)
    out = ys[dest[:, 0]] * wts[:, 0:1]
    for k in range(1, TOP_K):
        out = out + ys[dest[:, k]] * wts[:, k:k + 1]
    return out


def _rmsnorm(x, g):
    return x * lax.rsqrt(jnp.mean(x * x, axis=-1, keepdims=True) + NORM_EPS) * g


def _adaln(cvec, w_mod, b_mod, name):
    m = project(jax.nn.silu(cvec), w_mod, name) + b_mod
    return jnp.split(m, N_MOD, axis=-1)


def _centred_shift(y, mu):
    pad = jnp.pad(y, ((0, 0), (1, 1), (0, 0)))
    nb = 0.5 * (pad[:, :-2] + pad[:, 2:])
    return y + mu * (nb - y)


def _even_layout(d_rwkv, lora_w, lora_a, lora_g, d_gmlp):
    pad = lambda n: -(-n // LANES) * LANES
    d_shift = 3 * d_rwkv + lora_w + lora_a + lora_g
    src = {"rkv": (0, 3 * d_rwkv), "xw": (3 * d_rwkv, lora_w), "xa": (3 * d_rwkv + lora_w, lora_a),
           "xg": (3 * d_rwkv + lora_w + lora_a, lora_g), "gmlp": (d_shift, 2 * d_gmlp)}
    order = ["rkv", "gmlp", "xw", "xa", "xg"]
    dst, off = {}, 0
    for name in order:
        dst[name] = off
        off += pad(src[name][1])
    return src, dst, order, off


def _regroup_cols(w, src, dst, order, total):
    parts = []
    for name in order:
        s, n = src[name]
        blk = w[..., s:s + n]
        padn = -(-n // LANES) * LANES - n
        if padn:
            blk = jnp.pad(blk, [(0, 0)] * (w.ndim - 1) + [(0, padn)])
        parts.append(blk)
    return jnp.concatenate(parts, axis=-1)


def _even_mixer(nx, nc, p, need_ctx, tag):
    (w_in, mu, w0, w_up, a0, a_up, g_up, k_k, k_a, r_k, lnx_g, lnx_b, v_ln_g, v_ln_b, ws, bs, w_out) = p
    b, s, d = nx.shape
    l = nc.shape[1]
    d_rwkv = g_up.shape[1]
    heads = d_rwkv // RWKV_HEAD_DIM
    lora_w, lora_a, lora_g = w_up.shape[1], a_up.shape[1], g_up.shape[0]
    d_gmlp = v_ln_g.shape[0]
    src, dst, order, total = _even_layout(d_rwkv, lora_w, lora_a, lora_g, d_gmlp)
    w_in_r = _regroup_cols(w_in, src, dst, order, total)
    d_shift = 3 * d_rwkv + lora_w + lora_a + lora_g
    mu_r = _regroup_cols(jnp.concatenate([mu, jnp.zeros((2 * d_gmlp,), F32)]), src, dst, order, total)

    yx = project(nx.reshape(b * s, d), w_in_r, "even_in_x" + tag).reshape(b, s, total)
    yc = project(nc.reshape(b * l, d), w_in_r, "even_in_c" + tag).reshape(b, l, total)

    def prepare(y):
        ys = _centred_shift(y, mu_r)
        r = ys[..., 0:d_rwkv]
        k = ys[..., d_rwkv:2 * d_rwkv]
        v = ys[..., 2 * d_rwkv:3 * d_rwkv]
        xw = ys[..., dst["xw"]:dst["xw"] + lora_w]
        xa = ys[..., dst["xa"]:dst["xa"] + lora_a]
        xg = ys[..., dst["xg"]:dst["xg"] + lora_g]
        bt = y.shape[:2]
        hd = lambda t: t.reshape(*bt, heads, RWKV_HEAD_DIM)
        kk = hd(k * k_k)
        kk = (kk * lax.rsqrt(jnp.sum(kk * kk, axis=-1, keepdims=True) + 1e-12)).reshape(*bt, d_rwkv)
        g = jnp.dot(jax.nn.sigmoid(xg), g_up)
        lws, kds, ads = [], [], []
        for dd in range(2):
            w_log = -jax.nn.softplus(-(w0[dd] + jnp.dot(jnp.tanh(xw), w_up[dd]))) - 0.5
            a = jax.nn.sigmoid(a0[dd] + jnp.dot(xa, a_up[dd]))
            lws.append(-jnp.exp(w_log))
            kds.append(k * (1.0 + (a - 1.0) * k_a))
            ads.append(a)
        return r, v, kk, g, lws, kds, ads

    px = prepare(yx)
    pc = prepare(yc)
    cat = lambda a, bb: jnp.concatenate([a, bb], axis=1)
    r_all, v_all, kk_all, g_all = (cat(pc[i], px[i]) for i in range(4))
    y_sum = None
    for dd in range(2):
        y_d = wkv_scan(r_all, v_all, kk_all, cat(pc[4][dd], px[4][dd]), cat(pc[5][dd], px[5][dd]),
                       cat(pc[6][dd], px[6][dd]), n_ctx_chunks=l // WKV_CHUNK, reverse=dd == 1,
                       name="wkv%d%s" % (dd, tag))
        y_sum = y_d if y_sum is None else y_sum + y_d

    t_all = l + s
    hd = lambda t: t.reshape(b, t_all, heads, RWKV_HEAD_DIM)
    k0 = cat(pc[5][0], px[5][0])
    k1 = cat(pc[5][1], px[5][1])
    bonus = jnp.sum(hd(r_all) * hd(k0 + k1) * r_k, axis=-1, keepdims=True) * hd(v_all)
    wk = hd(y_sum)
    mean = jnp.mean(wk, axis=-1, keepdims=True)
    var = jnp.mean(jnp.square(wk - mean), axis=-1, keepdims=True)
    gn = ((wk - mean) * lax.rsqrt(var + RWKV_GN_EPS)).reshape(b, t_all, d_rwkv) * lnx_g + lnx_b
    rw_out = (gn + bonus.reshape(b, t_all, d_rwkv)) * g_all

    ub, vb = dst["gmlp"] // d_gmlp, dst["gmlp"] // d_gmlp + 1
    assert dst["gmlp"] % d_gmlp == 0
    gx = gmlp(yx, ub, vb, v_ln_g, v_ln_b, ws, bs, "gmlp_x" + tag)
    mix_x = project(jnp.concatenate([rw_out[:, l:], gx], axis=-1).reshape(b * s, -1), w_out,
                    "even_out_x" + tag).reshape(b, s, d)
    if not need_ctx:
        return mix_x, None
    gc = gmlp(yc, ub, vb, v_ln_g, v_ln_b, ws, bs, "gmlp_c" + tag)
    mix_c = project(jnp.concatenate([rw_out[:, :l], gc], axis=-1).reshape(b * l, -1), w_out,
                    "even_out_c" + tag).reshape(b, l, d)
    return mix_x, mix_c


def _rope_tables(rows):
    row = jnp.repeat(jnp.arange(rows, dtype=F32), GRID_W)
    col = jnp.tile(jnp.arange(GRID_W, dtype=F32), rows)
    n_freq = ATT_HEAD_DIM // 4
    inv = jnp.power(jnp.float32(ROPE_THETA), -jnp.arange(n_freq, dtype=F32) / n_freq)
    ang_r = (row[:, None] * inv)[:, None, :]
    ang_c = (col[:, None] * inv)[:, None, :]
    return jnp.cos(ang_r), jnp.sin(ang_r), jnp.cos(ang_c), jnp.sin(ang_c)


def _rotate(x, cos, sin):
    x1, x2 = jnp.split(x, 2, axis=-1)
    return jnp.concatenate([x1 * cos - x2 * sin, x1 * sin + x2 * cos], axis=-1)


def _axial_rope(x, rope):
    cos_r, sin_r, cos_c, sin_c = rope
    xr, xc = jnp.split(x, 2, axis=-1)
    return jnp.concatenate([_rotate(xr, cos_r, sin_r), _rotate(xc, cos_c, sin_c)], axis=-1)


def _odd_mixer(nx, nc, p, rope, need_ctx, tag):
    w_in, q_g, k_g, w_out = p
    b, s, d = nx.shape
    l = nc.shape[1]
    hd = ATT_HEAD_DIM
    n_q = w_out.shape[0] // hd
    n_kv = (w_in.shape[1] // hd - n_q) // 2

    def split(h, t, name):
        y = project(h.reshape(b * t, d), w_in, name).reshape(b, t, -1)
        q = _rmsnorm(y[..., :n_q * hd].reshape(b, t, n_q, hd), q_g)
        k = _rmsnorm(y[..., n_q * hd:(n_q + n_kv) * hd].reshape(b, t, n_kv, hd), k_g)
        return q, k, y[..., (n_q + n_kv) * hd:]

    qx, kx, vx = split(nx, s, "odd_in_x" + tag)
    qc, kc, vc = split(nc, l, "odd_in_c" + tag)
    qx = _axial_rope(qx, rope)
    kx = _axial_rope(kx, rope)
    scale = hd ** -0.5
    keys = jnp.concatenate([kc, kx], axis=1).reshape(b, l + s, n_kv * hd)
    vals = jnp.concatenate([vc, vx], axis=1)
    ox = attention((qx * scale).reshape(b, s, n_q * hd), keys, vals, n_kv=n_kv, name="attn_x" + tag)
    mix_x = project(ox.reshape(b * s, -1), w_out, "odd_out_x" + tag).reshape(b, s, d)
    if not need_ctx:
        return mix_x, None
    oc = attention((qc * scale).reshape(b, l, n_q * hd), kc.reshape(b, l, n_kv * hd), vc, n_kv=n_kv,
                   name="attn_c" + tag)
    mix_c = project(oc.reshape(b * l, -1), w_out, "odd_out_c" + tag).reshape(b, l, d)
    return mix_x, mix_c


def kernel(x, c, ctx, c_ctx, norm1_g, norm2_g, w_mod, b_mod, e_w_in, e_mu, e_w0, e_w_up, e_a0, e_a_up, e_g_up, e_k_k, e_k_a, e_r_k, e_lnx_g, e_lnx_b, e_v_ln_g, e_v_ln_b, e_ws, e_bs, e_w_out, o_w_in, o_q_g, o_k_g, o_w_out, router_w, router_b, moe_w1, moe_b1, moe_w2, moe_b2, final_g):
    b, s, d = x.shape
    l = ctx.shape[1]
    depth = norm1_g.shape[0]
    rope = _rope_tables(s // GRID_W)
    h_lat, h_ctx = x, ctx
    for i in range(depth):
        tag = "_l%d" % i
        need_ctx = i < depth - 1
        j = i // 2
        mods = _adaln(jnp.concatenate([c, c_ctx[None, :]], axis=0), w_mod[i], b_mod[i], "adaln" + tag)
        m_lat = [m[:b, None, :] for m in mods]
        m_ctx = [m[b:, None, :] for m in mods]
        nx = _rmsnorm(h_lat, norm1_g[i]) * (1.0 + m_lat[1]) + m_lat[0]
        nc = _rmsnorm(h_ctx, norm1_g[i]) * (1.0 + m_ctx[1]) + m_ctx[0]
        if i % 2 == 0:
            p = (e_w_in[j], e_mu[j], e_w0[j], e_w_up[j], e_a0[j], e_a_up[j], e_g_up[j], e_k_k[j], e_k_a[j],
                 e_r_k[j], e_lnx_g[j], e_lnx_b[j], e_v_ln_g[j], e_v_ln_b[j], e_ws[j], e_bs[j], e_w_out[j])
            mx, mc = _even_mixer(nx, nc, p, need_ctx, tag)
        else:
            mx, mc = _odd_mixer(nx, nc, (o_w_in[j], o_q_g[j], o_k_g[j], o_w_out[j]), rope, need_ctx, tag)
        h_lat = h_lat + m_lat[2] * mx
        nx = _rmsnorm(h_lat, norm2_g[i]) * (1.0 + m_lat[4]) + m_lat[3]
        if need_ctx:
            h_ctx = h_ctx + m_ctx[2] * mc
            nc = _rmsnorm(h_ctx, norm2_g[i]) * (1.0 + m_ctx[4]) + m_ctx[3]
            tokens = jnp.concatenate([nx.reshape(b * s, d), nc.reshape(b * l, d)], axis=0)
        else:
            tokens = nx.reshape(b * s, d)
        f = moe(tokens, router_w[i], router_b[i], moe_w1, moe_b1[i], moe_w2, moe_b2[i], i, tag)
        h_lat = h_lat + m_lat[5] * f[:b * s].reshape(b, s, d)
        if need_ctx:
            h_ctx = h_ctx + m_ctx[5] * f[b * s:].reshape(b, l, d)
    return _rmsnorm(h_lat, final_g)
```

```python
import functools

import jax
import jax.numpy as jnp
from jax import lax
from jax.experimental import pallas as pl
from jax.experimental.pallas import tpu as pltpu

F32 = jnp.float32
BF16 = jnp.bfloat16

NORM_EPS = 1e-6
N_MOD = 6
RWKV_HEAD_DIM = 64
RWKV_GN_EPS = 64e-5
GMLP_LN_EPS = 1e-5
ATT_HEAD_DIM = 128
GRID_W = 64
ROPE_THETA = 10000.0
TOP_K = 4
SWIGLU_ALPHA = 1.702
SWIGLU_LIMIT = 7.0

LANES = 128
VMEM_LIMIT_BYTES = 56 * 1024 * 1024

WKV_CHUNK = 64


def _params(*sem):
    return pltpu.CompilerParams(dimension_semantics=sem, vmem_limit_bytes=VMEM_LIMIT_BYTES)


def _dot(a, b):
    return jnp.dot(a.astype(BF16), b.astype(BF16), preferred_element_type=F32)


def _dot_nt(a, b):
    return lax.dot_general(a.astype(BF16), b.astype(BF16), (((1,), (1,)), ((), ())),
                           preferred_element_type=F32)


def _dot_tn(a, b):
    return lax.dot_general(a.astype(BF16), b.astype(BF16), (((0,), (0,)), ((), ())),
                           preferred_element_type=F32)


def _mm_body(x_ref, w_ref, o_ref):
    o_ref[...] = _dot(x_ref[...], w_ref[...])


def matmul(x, w, *, tm, tn, name):
    m, k = x.shape
    n = w.shape[1]
    assert m % tm == 0 and n % tn == 0, (m, n, tm, tn)
    return pl.pallas_call(
        _mm_body,
        grid=(m // tm, n // tn),
        in_specs=[pl.BlockSpec((tm, k), lambda i, j: (i, 0)),
                  pl.BlockSpec((k, tn), lambda i, j: (0, j))],
        out_specs=pl.BlockSpec((tm, tn), lambda i, j: (i, j)),
        out_shape=jax.ShapeDtypeStruct((m, n), F32),
        compiler_params=_params("parallel", "arbitrary"),
        name=name,
    )(x, w)


def _row_tile(m, cap):
    t = cap
    while m % t:
        t //= 2
    return t


def project(x2d, w, name):
    m = x2d.shape[0]
    mp = -(-m // 8) * 8
    if mp != m:
        x2d = jnp.pad(x2d, ((0, mp - m), (0, 0)))
    out = matmul(x2d, w, tm=_row_tile(mp, 1024), tn=_row_tile(w.shape[1], 512), name=name)
    return out[:m]


def _cumsum_rows(tri_bf16, x):
    hi = x.astype(BF16)
    r1 = x - hi.astype(F32)
    mid = r1.astype(BF16)
    lo = (r1 - mid.astype(F32)).astype(BF16)
    parts = jnp.concatenate([hi, mid, lo], axis=1)
    acc = jnp.dot(tri_bf16, parts, preferred_element_type=F32)
    n = x.shape[1]
    return acc[:, :n] + acc[:, n:2 * n] + acc[:, 2 * n:]


def _bdot(a, b):
    return jnp.dot(a, b, preferred_element_type=F32)


def _wkv_body(r_ref, v_ref, kk_ref, lw_ref, k_ref, a_ref, y_ref, s_ref, *, reverse, n_pairs):
    c = WKV_CHUNK
    w2 = 2 * c

    @pl.when(pl.program_id(1) == 0)
    def _():
        s_ref[...] = jnp.zeros_like(s_ref)

    row = lax.broadcasted_iota(jnp.int32, (w2, w2), 0)
    col = lax.broadcasted_iota(jnp.int32, (w2, w2), 1)
    row_a = row < c
    same = row_a == (col < c)
    tt = jnp.where(row_a, row, row - c)
    ss = jnp.where(col < c, col, col - c)
    before = (ss > tt) if reverse else (ss < tt)
    strict = same & before
    incl = same & (before | (ss == tt))
    r64 = lax.broadcasted_iota(jnp.int32, (c, c), 0)
    c64 = lax.broadcasted_iota(jnp.int32, (c, c), 1)
    tri = ((c64 >= r64) if reverse else (c64 <= r64)).astype(BF16)
    lane_a = lax.broadcasted_iota(jnp.int32, (c, w2), 1) < c
    last = 0 if reverse else c - 1
    pairs = range(n_pairs)
    sls = [slice(p * w2, (p + 1) * w2) for p in pairs]
    zero = jnp.zeros((c, w2), F32)
    zero2 = jnp.zeros((w2, w2), F32)

    def split_heads(x):
        return jnp.concatenate([jnp.where(lane_a, x, zero), jnp.where(lane_a, zero, x)], axis=0)

    lw = [lw_ref[0, :, sl] for sl in sls]
    big_l = [_cumsum_rows(tri, lw[p]) for p in pairs]
    l_tot = [big_l[p][last:last + 1, :] for p in pairs]

    kk = [kk_ref[0, :, sl] for sl in sls]
    kd = [k_ref[0, :, sl] for sl in sls]
    bb = [kk[p] * a_ref[0, :, sls[p]] for p in pairs]
    vv = [v_ref[0, :, sl] for sl in sls]
    e_neg = [jnp.exp(-big_l[p]) for p in pairs]
    q_t = [kk[p] * jnp.exp(big_l[p] - lw[p]) for p in pairs]
    r_t = [r_ref[0, :, sls[p]] * jnp.exp(big_l[p]) for p in pairs]
    k_t = [(kd[p] * e_neg[p]).astype(BF16) for p in pairs]
    b_t = [(bb[p] * e_neg[p]).astype(BF16) for p in pairs]
    e_rest = [jnp.exp(l_tot[p] - big_l[p]) for p in pairs]
    kb_p = [jnp.concatenate([kd[p] * e_rest[p], bb[p] * e_rest[p]], axis=0).astype(BF16) for p in pairs]

    lhs4 = [jnp.concatenate([split_heads(q_t[p]), split_heads(r_t[p])], axis=0).astype(BF16) for p in pairs]
    s_bd = [s_ref[p] for p in pairs]
    big = [lax.dot_general(lhs4[p],
                           jnp.concatenate([b_t[p], k_t[p], k_t[p], b_t[p], s_bd[p].astype(BF16)], axis=0),
                           (((1,), (1,)), ((), ())), preferred_element_type=F32) for p in pairs]
    n_bd, m_both, nr_bd = [], [], []
    for p in pairs:
        q1, r1 = big[p][:w2, :w2], big[p][w2:, :w2]
        q2, r2 = big[p][:w2, w2:2 * w2], big[p][w2:, w2:2 * w2]
        n_bd.append(jnp.where(strict, jnp.where(row_a, q1, q2), zero2).astype(BF16))
        nr_bd.append(jnp.where(incl, jnp.where(row_a, r1, r2), zero2).astype(BF16))
        m_both.append(jnp.concatenate([jnp.where(strict, jnp.where(row_a, q2, q1), zero2),
                                       jnp.where(incl, jnp.where(row_a, r2, r1), zero2)], axis=0).astype(BF16))
    mv = [_bdot(m_both[p], split_heads(vv[p]).astype(BF16)) for p in pairs]

    x = [big[p][:w2, 2 * w2:] + mv[p][:w2] for p in pairs]
    pw = n_bd
    span = 1
    while span < c:
        if 2 * span < c:
            both = [_bdot(pw[p], jnp.concatenate([x[p].astype(BF16), pw[p]], axis=1)) for p in pairs]
            pw = [both[p][:, w2:].astype(BF16) for p in pairs]
        else:
            both = [_bdot(pw[p], x[p].astype(BF16)) for p in pairs]
        x = [(x[p] - both[p][:, :w2]) if span == 1 else (x[p] + both[p][:, :w2]) for p in pairs]
        span *= 2
    u_bs = x

    nru = [_bdot(nr_bd[p], u_bs[p].astype(BF16)) for p in pairs]
    for p in pairs:
        y_bs = big[p][w2:, 2 * w2:] + mv[p][w2:] - nru[p]
        y_ref[0, :, sls[p]] = y_bs[:c] + y_bs[c:]

    vu = [jnp.concatenate([vv[p], -(u_bs[p][:c] + u_bs[p][c:])], axis=0).astype(BF16) for p in pairs]
    upd = [lax.dot_general(vu[p], kb_p[p], (((0,), (0,)), ((), ())), preferred_element_type=F32)
           for p in pairs]
    for p in pairs:
        s_ref[p] = s_bd[p] * jnp.exp(l_tot[p]) + jnp.where(same, upd[p], zero2)


def wkv_scan(r, v, kk, lw, kd, ad, *, n_ctx_chunks, reverse, name):
    b, t, d = r.shape
    c = WKV_CHUNK
    nc = t // c
    lc = n_ctx_chunks
    n_pairs = d // (2 * RWKV_HEAD_DIM)

    if reverse:
        def pos(s):
            return jnp.where(s < lc, lc - 1 - s, nc - 1 - s + lc)
    else:
        def pos(s):
            return s
    spec = pl.BlockSpec((1, c, d), lambda i, s: (i, pos(s), 0))
    return pl.pallas_call(
        functools.partial(_wkv_body, reverse=reverse, n_pairs=n_pairs),
        grid=(b, nc),
        in_specs=[spec] * 6,
        out_specs=spec,
        out_shape=jax.ShapeDtypeStruct((b, t, d), F32),
        scratch_shapes=[pltpu.VMEM((n_pairs, 2 * c, 2 * c), F32)],
        compiler_params=_params("parallel", "arbitrary"),
        name=name,
    )(r, v, kk, lw, kd, ad)


def _gelu(x):
    return 0.5 * x * (1.0 + lax.erf(x * (2.0 ** -0.5)))


def _gmlp_body(u_ref, v_ref, g_ref, b_ref, ws_ref, bs_ref, o_ref, *, groups):
    u = _gelu(u_ref[0])
    v = _gelu(v_ref[0])
    mu = jnp.mean(v, axis=-1, keepdims=True)
    var = jnp.mean(jnp.square(v - mu), axis=-1, keepdims=True)
    vn = (v - mu) * lax.rsqrt(var + GMLP_LN_EPS) * g_ref[...] + b_ref[...]
    gd = vn.shape[1] // groups
    for g in range(groups):
        sl = slice(g * gd, (g + 1) * gd)
        mixed = _dot(ws_ref[g], vn[:, sl]) + bs_ref[g]
        o_ref[0, :, sl] = u[:, sl] * mixed


def gmlp(y, u_block, v_block, ln_g, ln_b, ws, bs, name):
    b, t, _ = y.shape
    groups, chunk, _ = ws.shape
    dg = ln_g.shape[0]
    gd = dg // groups
    bs_b = jnp.broadcast_to(bs[:, :, None], (groups, chunk, gd))
    return pl.pallas_call(
        functools.partial(_gmlp_body, groups=groups),
        grid=(b, t // chunk),
        in_specs=[pl.BlockSpec((1, chunk, dg), lambda i, n: (i, n, u_block)),
                  pl.BlockSpec((1, chunk, dg), lambda i, n: (i, n, v_block)),
                  pl.BlockSpec((1, dg), lambda i, n: (0, 0)),
                  pl.BlockSpec((1, dg), lambda i, n: (0, 0)),
                  pl.BlockSpec((groups, chunk, chunk), lambda i, n: (0, 0, 0)),
                  pl.BlockSpec((groups, chunk, gd), lambda i, n: (0, 0, 0))],
        out_specs=pl.BlockSpec((1, chunk, dg), lambda i, n: (i, n, 0)),
        out_shape=jax.ShapeDtypeStruct((b, t, dg), F32),
        compiler_params=_params("parallel", "parallel"),
        name=name,
    )(y, y, ln_g[None, :], ln_b[None, :], ws, bs_b)


def _attn_body(q_ref, k_ref, v_ref, o_ref, *, group):
    k = k_ref[0].astype(BF16)
    v = v_ref[0].astype(BF16)
    hd = k.shape[1]
    for g in range(group):
        sl = slice(g * hd, (g + 1) * hd)
        s = _dot_nt(q_ref[0, :, sl], k)
        p = jnp.exp(s - jnp.max(s, axis=-1, keepdims=True))
        denom = jnp.sum(p, axis=-1, keepdims=True)
        o_ref[0, :, sl] = jnp.dot(p.astype(BF16), v, preferred_element_type=F32) / denom


def attention(q, k, v, *, n_kv, name):
    b, tq, dq = q.shape
    tk = k.shape[1]
    hd = k.shape[2] // n_kv
    group = dq // hd // n_kv
    bq = _row_tile(tq, 256)
    return pl.pallas_call(
        functools.partial(_attn_body, group=group),
        grid=(b, n_kv, tq // bq),
        in_specs=[pl.BlockSpec((1, bq, group * hd), lambda i, h, j: (i, j, h)),
                  pl.BlockSpec((1, tk, hd), lambda i, h, j: (i, 0, h)),
                  pl.BlockSpec((1, tk, hd), lambda i, h, j: (i, 0, h))],
        out_specs=pl.BlockSpec((1, bq, group * hd), lambda i, h, j: (i, j, h)),
        out_shape=jax.ShapeDtypeStruct((b, tq, dq), F32),
        compiler_params=_params("parallel", "parallel", "arbitrary"),
        name=name,
    )(q, k, v)


def _router_body(h_ref, w_ref, b_ref, idx_ref, wt_ref, pos_ref, cnt_ref, run_ref):
    logits = jnp.dot(h_ref[...], w_ref[...], precision=lax.Precision.HIGHEST,
                     preferred_element_type=F32) + b_ref[...]
    n_e = logits.shape[1]
    lane = lax.broadcasted_iota(jnp.int32, logits.shape, 1)
    vals, ids = [], []
    cur = logits
    for _ in range(TOP_K):
        m = jnp.max(cur, axis=1, keepdims=True)
        sel = jnp.min(jnp.where(cur == m, lane, n_e), axis=1, keepdims=True)
        vals.append(m)
        ids.append(sel)
        cur = jnp.where(lane == sel, -jnp.inf, cur)
    e = jnp.exp(jnp.concatenate(vals, axis=1) - vals[0])
    idx_ref[...] = jnp.concatenate(ids, axis=1)
    wt_ref[...] = e / jnp.sum(e, axis=1, keepdims=True)

    @pl.when(pl.program_id(0) == 0)
    def _():
        run_ref[...] = jnp.zeros_like(run_ref)

    tm = logits.shape[0]
    hot = [jnp.where(lane == sel, 1.0, 0.0) for sel in ids]
    chosen = hot[0] + hot[1] + hot[2] + hot[3]
    tri = jnp.where(lax.broadcasted_iota(jnp.int32, (tm, tm), 1) < lax.broadcasted_iota(jnp.int32, (tm, tm), 0),
                    1.0, 0.0).astype(BF16)
    base = run_ref[...] + jnp.dot(tri, chosen.astype(BF16), preferred_element_type=F32)
    pos_ref[...] = jnp.concatenate([jnp.sum(h * base, axis=1, keepdims=True) for h in hot],
                                   axis=1).astype(jnp.int32)
    run_ref[...] += jnp.sum(chosen, axis=0, keepdims=True)
    cnt_ref[...] = run_ref[...]


def router(h, rw, rb, name):
    t, d = h.shape
    n_e = rw.shape[1]
    tm = _row_tile(t, 256)
    return pl.pallas_call(
        _router_body,
        grid=(t // tm,),
        in_specs=[pl.BlockSpec((tm, d), lambda i: (i, 0)),
                  pl.BlockSpec((d, n_e), lambda i: (0, 0)),
                  pl.BlockSpec((1, n_e), lambda i: (0, 0))],
        out_specs=[pl.BlockSpec((tm, TOP_K), lambda i: (i, 0)),
                   pl.BlockSpec((tm, TOP_K), lambda i: (i, 0)),
                   pl.BlockSpec((tm, TOP_K), lambda i: (i, 0)),
                   pl.BlockSpec((1, n_e), lambda i: (0, 0))],
        out_shape=[jax.ShapeDtypeStruct((t, TOP_K), jnp.int32),
                   jax.ShapeDtypeStruct((t, TOP_K), F32),
                   jax.ShapeDtypeStruct((t, TOP_K), jnp.int32),
                   jax.ShapeDtypeStruct((1, n_e), F32)],
        scratch_shapes=[pltpu.VMEM((1, n_e), F32)],
        compiler_params=_params("arbitrary"),
        name=name,
    )(h, rw, rb[None, :])


def _dispatch_body(dest_ref, x_ref, init_ref, xs_ref, sem, *, rows):
    del init_ref

    def row_copy(r, j):
        return pltpu.make_async_copy(x_ref.at[pl.ds(r, 1)], xs_ref.at[pl.ds(dest_ref[0, 0, r * TOP_K + j], 1)], sem)

    def issue(r, carry):
        for j in range(TOP_K):
            row_copy(r, j).start()
        return carry

    def drain(r, carry):
        for j in range(TOP_K):
            row_copy(r, j).wait()
        return carry

    lax.fori_loop(0, rows, issue, 0, unroll=8)
    lax.fori_loop(0, rows, drain, 0, unroll=8)


def dispatch(tokens, dest, np_rows, name):
    t, d = tokens.shape
    rows = _row_tile(t, 256)
    n_tiles = t // rows
    return pl.pallas_call(
        functools.partial(_dispatch_body, rows=rows),
        grid=(n_tiles,),
        in_specs=[pl.BlockSpec((1, 1, rows * TOP_K), lambda i: (i, 0, 0), memory_space=pltpu.SMEM),
                  pl.BlockSpec((rows, d), lambda i: (i, 0)),
                  pl.BlockSpec(memory_space=pl.ANY)],
        out_specs=pl.BlockSpec(memory_space=pl.ANY),
        out_shape=jax.ShapeDtypeStruct((np_rows, d), F32),
        scratch_shapes=[pltpu.SemaphoreType.DMA],
        input_output_aliases={2: 0},
        compiler_params=_params("arbitrary"),
        name=name,
    )(dest.reshape(n_tiles, 1, rows * TOP_K), tokens, jnp.zeros((np_rows, d), F32))


def _ffn_body(te_ref, nv_ref, x_ref, wg_ref, wl_ref, bg_ref, bl_ref, w2_ref, b2_ref, o_ref, xb_ref):
    i = pl.program_id(0)
    c = pl.program_id(1)
    valid = i < nv_ref[0]

    @pl.when(valid & (c == 0))
    def _():
        xb_ref[...] = x_ref[...].astype(BF16)

    @pl.when(valid)
    def _():
        x = xb_ref[...]
        glu = _dot(x, wg_ref[0]) + bg_ref[0]
        lin = _dot(x, wl_ref[0]) + bl_ref[0]
        glu = jnp.minimum(glu, SWIGLU_LIMIT)
        lin = jnp.clip(lin, -SWIGLU_LIMIT, SWIGLU_LIMIT)
        act = glu * jax.nn.sigmoid(SWIGLU_ALPHA * glu) * (lin + 1.0)
        part = _dot(act, w2_ref[0])

        @pl.when(c == 0)
        def _():
            o_ref[...] = part + b2_ref[0]

        @pl.when(c > 0)
        def _():
            o_ref[...] += part

    @pl.when(jnp.logical_not(valid) & (c == 0))
    def _():
        o_ref[...] = jnp.zeros_like(o_ref)


def moe_ffn(xs, tile_expert, n_valid, w1, b1, w2, b2, *, layer, tm, fc, name):
    np_rows, d = xs.shape
    f = w2.shape[2]
    nfc = f // fc
    n_tiles = np_rows // tm
    grid_spec = pltpu.PrefetchScalarGridSpec(
        num_scalar_prefetch=2,
        grid=(n_tiles, nfc),
        in_specs=[pl.BlockSpec((tm, d), lambda i, c, te, nv: (i, 0)),
                  pl.BlockSpec((None, 1, d, fc), lambda i, c, te, nv: (layer, te[i], 0, c)),
                  pl.BlockSpec((None, 1, d, fc), lambda i, c, te, nv: (layer, te[i], 0, nfc + c)),
                  pl.BlockSpec((1, 1, fc), lambda i, c, te, nv: (te[i], 0, c)),
                  pl.BlockSpec((1, 1, fc), lambda i, c, te, nv: (te[i], 0, nfc + c)),
                  pl.BlockSpec((None, 1, fc, d), lambda i, c, te, nv: (layer, te[i], c, 0)),
                  pl.BlockSpec((1, 1, d), lambda i, c, te, nv: (te[i], 0, 0))],
        out_specs=pl.BlockSpec((tm, d), lambda i, c, te, nv: (i, 0)),
        scratch_shapes=[pltpu.VMEM((tm, d), BF16)],
    )
    return pl.pallas_call(
        _ffn_body,
        grid_spec=grid_spec,
        out_shape=jax.ShapeDtypeStruct((np_rows, d), F32),
        compiler_params=_params("arbitrary", "arbitrary"),
        name=name,
    )(tile_expert, n_valid, xs, w1, w1, b1, b1, w2, b2)


def moe(tokens, rw, rb, w1, b1, w2, b2, layer, tag):
    t, d = tokens.shape
    n_e = rw.shape[1]
    tm = 1024 if t * TOP_K >= 1024 * n_e else 128
    fc = 256
    idx, wts, pos, cnt = router(tokens, rw, rb, "router" + tag)

    n_pairs = t * TOP_K
    counts = cnt[0].astype(jnp.int32)
    padded = (counts + tm - 1) // tm * tm
    ends_p = jnp.cumsum(padded)
    starts_p = ends_p - padded
    e_ids = jnp.arange(n_e, dtype=jnp.int32)
    dest = pos + jnp.sum(jnp.where(idx[:, :, None] == e_ids, starts_p, 0), axis=-1)
    n_tiles = -(-n_pairs // tm) + n_e
    np_rows = n_tiles * tm
    n_valid = (ends_p[-1] // tm).astype(jnp.int32)
    tile_start = jnp.arange(n_tiles, dtype=jnp.int32) * tm
    tile_expert = jnp.sum((tile_start[:, None] >= ends_p[None, :]).astype(jnp.int32), axis=1)
    last_e = jnp.sum((ends_p[-1] - 1 >= ends_p).astype(jnp.int32))
    tile_expert = jnp.minimum(tile_expert, last_e)

    xs = dispatch(tokens, dest, np_rows, "dispatch" + tag)
    ys = moe_ffn(xs, tile_expert, n_valid[None], w1, b1[:, None, :], w2, b2[:, None, :], layer=layer,
                 tm=tm, fc=fc, name="moe_ffn" + tag)
    out = ys[dest[:, 0]] * wts[:, 0:1]
    for k in range(1, TOP_K):
        out = out + ys[dest[:, k]] * wts[:, k:k + 1]
    return out


def _rmsnorm(x, g):
    return x * lax.rsqrt(jnp.mean(x * x, axis=-1, keepdims=True) + NORM_EPS) * g


def _adaln(cvec, w_mod, b_mod, name):
    m = project(jax.nn.silu(cvec), w_mod, name) + b_mod
    return jnp.split(m, N_MOD, axis=-1)


def _centred_shift(y, mu):
    pad = jnp.pad(y, ((0, 0), (1, 1), (0, 0)))
    nb = 0.5 * (pad[:, :-2] + pad[:, 2:])
    return y + mu * (nb - y)


def _even_layout(d_rwkv, lora_w, lora_a, lora_g, d_gmlp):
    pad = lambda n: -(-n // LANES) * LANES
    d_shift = 3 * d_rwkv + lora_w + lora_a + lora_g
    src = {"rkv": (0, 3 * d_rwkv), "xw": (3 * d_rwkv, lora_w), "xa": (3 * d_rwkv + lora_w, lora_a),
           "xg": (3 * d_rwkv + lora_w + lora_a, lora_g), "gmlp": (d_shift, 2 * d_gmlp)}
    order = ["rkv", "gmlp", "xw", "xa", "xg"]
    dst, off = {}, 0
    for name in order:
        dst[name] = off
        off += pad(src[name][1])
    return src, dst, order, off


def _regroup_cols(w, src, dst, order, total):
    parts = []
    for name in order:
        s, n = src[name]
        blk = w[..., s:s + n]
        padn = -(-n // LANES) * LANES - n
        if padn:
            blk = jnp.pad(blk, [(0, 0)] * (w.ndim - 1) + [(0, padn)])
        parts.append(blk)
    return jnp.concatenate(parts, axis=-1)


def _even_mixer(nx, nc, p, need_ctx, tag):
    (w_in, mu, w0, w_up, a0, a_up, g_up, k_k, k_a, r_k, lnx_g, lnx_b, v_ln_g, v_ln_b, ws, bs, w_out) = p
    b, s, d = nx.shape
    l = nc.shape[1]
    d_rwkv = g_up.shape[1]
    heads = d_rwkv // RWKV_HEAD_DIM
    lora_w, lora_a, lora_g = w_up.shape[1], a_up.shape[1], g_up.shape[0]
    d_gmlp = v_ln_g.shape[0]
    src, dst, order, total = _even_layout(d_rwkv, lora_w, lora_a, lora_g, d_gmlp)
    w_in_r = _regroup_cols(w_in, src, dst, order, total)
    d_shift = 3 * d_rwkv + lora_w + lora_a + lora_g
    mu_r = _regroup_cols(jnp.concatenate([mu, jnp.zeros((2 * d_gmlp,), F32)]), src, dst, order, total)

    yx = project(nx.reshape(b * s, d), w_in_r, "even_in_x" + tag).reshape(b, s, total)
    yc = project(nc.reshape(b * l, d), w_in_r, "even_in_c" + tag).reshape(b, l, total)

    def prepare(y):
        ys = _centred_shift(y, mu_r)
        r = ys[..., 0:d_rwkv]
        k = ys[..., d_rwkv:2 * d_rwkv]
        v = ys[..., 2 * d_rwkv:3 * d_rwkv]
        xw = ys[..., dst["xw"]:dst["xw"] + lora_w]
        xa = ys[..., dst["xa"]:dst["xa"] + lora_a]
        xg = ys[..., dst["xg"]:dst["xg"] + lora_g]
        bt = y.shape[:2]
        hd = lambda t: t.reshape(*bt, heads, RWKV_HEAD_DIM)
        kk = hd(k * k_k)
        kk = (kk * lax.rsqrt(jnp.sum(kk * kk, axis=-1, keepdims=True) + 1e-12)).reshape(*bt, d_rwkv)
        g = jnp.dot(jax.nn.sigmoid(xg), g_up)
        lws, kds, ads = [], [], []
        for dd in range(2):
            w_log = -jax.nn.softplus(-(w0[dd] + jnp.dot(jnp.tanh(xw), w_up[dd]))) - 0.5
            a = jax.nn.sigmoid(a0[dd] + jnp.dot(xa, a_up[dd]))
            lws.append(-jnp.exp(w_log))
            kds.append(k * (1.0 + (a - 1.0) * k_a))
            ads.append(a)
        return r, v, kk, g, lws, kds, ads

    px = prepare(yx)
    pc = prepare(yc)
    cat = lambda a, bb: jnp.concatenate([a, bb], axis=1)
    r_all, v_all, kk_all, g_all = (cat(pc[i], px[i]) for i in range(4))
    y_sum = None
    for dd in range(2):
        y_d = wkv_scan(r_all, v_all, kk_all, cat(pc[4][dd], px[4][dd]), cat(pc[5][dd], px[5][dd]),
                       cat(pc[6][dd], px[6][dd]), n_ctx_chunks=l // WKV_CHUNK, reverse=dd == 1,
                       name="wkv%d%s" % (dd, tag))
        y_sum = y_d if y_sum is None else y_sum + y_d

    t_all = l + s
    hd = lambda t: t.reshape(b, t_all, heads, RWKV_HEAD_DIM)
    k0 = cat(pc[5][0], px[5][0])
    k1 = cat(pc[5][1], px[5][1])
    bonus = jnp.sum(hd(r_all) * hd(k0 + k1) * r_k, axis=-1, keepdims=True) * hd(v_all)
    wk = hd(y_sum)
    mean = jnp.mean(wk, axis=-1, keepdims=True)
    var = jnp.mean(jnp.square(wk - mean), axis=-1, keepdims=True)
    gn = ((wk - mean) * lax.rsqrt(var + RWKV_GN_EPS)).reshape(b, t_all, d_rwkv) * lnx_g + lnx_b
    rw_out = (gn + bonus.reshape(b, t_all, d_rwkv)) * g_all

    ub, vb = dst["gmlp"] // d_gmlp, dst["gmlp"] // d_gmlp + 1
    assert dst["gmlp"] % d_gmlp == 0
    gx = gmlp(yx, ub, vb, v_ln_g, v_ln_b, ws, bs, "gmlp_x" + tag)
    mix_x = project(jnp.concatenate([rw_out[:, l:], gx], axis=-1).reshape(b * s, -1), w_out,
                    "even_out_x" + tag).reshape(b, s, d)
    if not need_ctx:
        return mix_x, None
    gc = gmlp(yc, ub, vb, v_ln_g, v_ln_b, ws, bs, "gmlp_c" + tag)
    mix_c = project(jnp.concatenate([rw_out[:, :l], gc], axis=-1).reshape(b * l, -1), w_out,
                    "even_out_c" + tag).reshape(b, l, d)
    return mix_x, mix_c


def _rope_tables(rows):
    row = jnp.repeat(jnp.arange(rows, dtype=F32), GRID_W)
    col = jnp.tile(jnp.arange(GRID_W, dtype=F32), rows)
    n_freq = ATT_HEAD_DIM // 4
    inv = jnp.power(jnp.float32(ROPE_THETA), -jnp.arange(n_freq, dtype=F32) / n_freq)
    ang_r = (row[:, None] * inv)[:, None, :]
    ang_c = (col[:, None] * inv)[:, None, :]
    return jnp.cos(ang_r), jnp.sin(ang_r), jnp.cos(ang_c), jnp.sin(ang_c)


def _rotate(x, cos, sin):
    x1, x2 = jnp.split(x, 2, axis=-1)
    return jnp.concatenate([x1 * cos - x2 * sin, x1 * sin + x2 * cos], axis=-1)


def _axial_rope(x, rope):
    cos_r, sin_r, cos_c, sin_c = rope
    xr, xc = jnp.split(x, 2, axis=-1)
    return jnp.concatenate([_rotate(xr, cos_r, sin_r), _rotate(xc, cos_c, sin_c)], axis=-1)


def _odd_mixer(nx, nc, p, rope, need_ctx, tag):
    w_in, q_g, k_g, w_out = p
    b, s, d = nx.shape
    l = nc.shape[1]
    hd = ATT_HEAD_DIM
    n_q = w_out.shape[0] // hd
    n_kv = (w_in.shape[1] // hd - n_q) // 2

    def split(h, t, name):
        y = project(h.reshape(b * t, d), w_in, name).reshape(b, t, -1)
        q = _rmsnorm(y[..., :n_q * hd].reshape(b, t, n_q, hd), q_g)
        k = _rmsnorm(y[..., n_q * hd:(n_q + n_kv) * hd].reshape(b, t, n_kv, hd), k_g)
        return q, k, y[..., (n_q + n_kv) * hd:]

    qx, kx, vx = split(nx, s, "odd_in_x" + tag)
    qc, kc, vc = split(nc, l, "odd_in_c" + tag)
    qx = _axial_rope(qx, rope)
    kx = _axial_rope(kx, rope)
    scale = hd ** -0.5
    keys = jnp.concatenate([kc, kx], axis=1).reshape(b, l + s, n_kv * hd)
    vals = jnp.concatenate([vc, vx], axis=1)
    ox = attention((qx * scale).reshape(b, s, n_q * hd), keys, vals, n_kv=n_kv, name="attn_x" + tag)
    mix_x = project(ox.reshape(b * s, -1), w_out, "odd_out_x" + tag).reshape(b, s, d)
    if not need_ctx:
        return mix_x, None
    oc = attention((qc * scale).reshape(b, l, n_q * hd), kc.reshape(b, l, n_kv * hd), vc, n_kv=n_kv,
                   name="attn_c" + tag)
    mix_c = project(oc.reshape(b * l, -1), w_out, "odd_out_c" + tag).reshape(b, l, d)
    return mix_x, mix_c


def kernel(x, c, ctx, c_ctx, norm1_g, norm2_g, w_mod, b_mod, e_w_in, e_mu, e_w0, e_w_up, e_a0, e_a_up, e_g_up, e_k_k, e_k_a, e_r_k, e_lnx_g, e_lnx_b, e_v_ln_g, e_v_ln_b, e_ws, e_bs, e_w_out, o_w_in, o_q_g, o_k_g, o_w_out, router_w, router_b, moe_w1, moe_b1, moe_w2, moe_b2, final_g):
    b, s, d = x.shape
    l = ctx.shape[1]
    depth = norm1_g.shape[0]
    rope = _rope_tables(s // GRID_W)
    h_lat, h_ctx = x, ctx
    for i in range(depth):
        tag = "_l%d" % i
        need_ctx = i < depth - 1
        j = i // 2
        mods = _adaln(jnp.concatenate([c, c_ctx[None, :]], axis=0), w_mod[i], b_mod[i], "adaln" + tag)
        m_lat = [m[:b, None, :] for m in mods]
        m_ctx = [m[b:, None, :] for m in mods]
        nx = _rmsnorm(h_lat, norm1_g[i]) * (1.0 + m_lat[1]) + m_lat[0]
        nc = _rmsnorm(h_ctx, norm1_g[i]) * (1.0 + m_ctx[1]) + m_ctx[0]
        if i % 2 == 0:
            p = (e_w_in[j], e_mu[j], e_w0[j], e_w_up[j], e_a0[j], e_a_up[j], e_g_up[j], e_k_k[j], e_k_a[j],
                 e_r_k[j], e_lnx_g[j], e_lnx_b[j], e_v_ln_g[j], e_v_ln_b[j], e_ws[j], e_bs[j], e_w_out[j])
            mx, mc = _even_mixer(nx, nc, p, need_ctx, tag)
        else:
            mx, mc = _odd_mixer(nx, nc, (o_w_in[j], o_q_g[j], o_k_g[j], o_w_out[j]), rope, need_ctx, tag)
        h_lat = h_lat + m_lat[2] * mx
        nx = _rmsnorm(h_lat, norm2_g[i]) * (1.0 + m_lat[4]) + m_lat[3]
        if need_ctx:
            h_ctx = h_ctx + m_ctx[2] * mc
            nc = _rmsnorm(h_ctx, norm2_g[i]) * (1.0 + m_ctx[4]) + m_ctx[3]
            tokens = jnp.concatenate([nx.reshape(b * s, d), nc.reshape(b * l, d)], axis=0)
        else:
            tokens = nx.reshape(b * s, d)
        f = moe(tokens, router_w[i], router_b[i], moe_w1, moe_b1[i], moe_w2, moe_b2[i], i, tag)
        h_lat = h_lat + m_lat[5] * f[:b * s].reshape(b, s, d)
        if need_ctx:
            h_ctx = h_ctx + m_ctx[5] * f[b * s:].reshape(b, l, d)
    return _rmsnorm(h_lat, final_g)
```

```python
import functools

import jax
import jax.numpy as jnp
from jax import lax
from jax.experimental import pallas as pl
from jax.experimental.pallas import tpu as pltpu

F32 = jnp.float32
BF16 = jnp.bfloat16

NORM_EPS = 1e-6
N_MOD = 6
RWKV_HEAD_DIM = 64
RWKV_GN_EPS = 64e-5
GMLP_LN_EPS = 1e-5
ATT_HEAD_DIM = 128
GRID_W = 64
ROPE_THETA = 10000.0
TOP_K = 4
SWIGLU_ALPHA = 1.702
SWIGLU_LIMIT = 7.0

LANES = 128
VMEM_LIMIT_BYTES = 56 * 1024 * 1024

WKV_CHUNK = 64


def _params(*sem):
    return pltpu.CompilerParams(dimension_semantics=sem, vmem_limit_bytes=VMEM_LIMIT_BYTES)


def _dot(a, b):
    return jnp.dot(a.astype(BF16), b.astype(BF16), preferred_element_type=F32)


def _dot_nt(a, b):
    return lax.dot_general(a.astype(BF16), b.astype(BF16), (((1,), (1,)), ((), ())),
                           preferred_element_type=F32)


def _dot_tn(a, b):
    return lax.dot_general(a.astype(BF16), b.astype(BF16), (((0,), (0,)), ((), ())),
                           preferred_element_type=F32)


def _mm_body(x_ref, w_ref, o_ref):
    o_ref[...] = _dot(x_ref[...], w_ref[...])


def matmul(x, w, *, tm, tn, name):
    m, k = x.shape
    n = w.shape[1]
    assert m % tm == 0 and n % tn == 0, (m, n, tm, tn)
    return pl.pallas_call(
        _mm_body,
        grid=(m // tm, n // tn),
        in_specs=[pl.BlockSpec((tm, k), lambda i, j: (i, 0)),
                  pl.BlockSpec((k, tn), lambda i, j: (0, j))],
        out_specs=pl.BlockSpec((tm, tn), lambda i, j: (i, j)),
        out_shape=jax.ShapeDtypeStruct((m, n), F32),
        compiler_params=_params("parallel", "arbitrary"),
        name=name,
    )(x, w)


def _row_tile(m, cap):
    t = cap
    while m % t:
        t //= 2
    return t


def project(x2d, w, name):
    m = x2d.shape[0]
    mp = -(-m // 8) * 8
    if mp != m:
        x2d = jnp.pad(x2d, ((0, mp - m), (0, 0)))
    out = matmul(x2d, w, tm=_row_tile(mp, 1024), tn=_row_tile(w.shape[1], 512), name=name)
    return out[:m]


def _mm2_body(x1_ref, x2_ref, w1_ref, w2_ref, o_ref):
    o_ref[...] = _dot(x1_ref[...], w1_ref[...]) + _dot(x2_ref[...], w2_ref[...])


def matmul2(x1, x2, w, name):
    m, k = x1.shape
    n = w.shape[1]
    assert x2.shape == (m, k) and w.shape[0] == 2 * k
    tm, tn = _row_tile(m, 1024), _row_tile(n, 512)
    return pl.pallas_call(
        _mm2_body,
        grid=(m // tm, n // tn),
        in_specs=[pl.BlockSpec((tm, k), lambda i, j: (i, 0)),
                  pl.BlockSpec((tm, k), lambda i, j: (i, 0)),
                  pl.BlockSpec((k, tn), lambda i, j: (0, j)),
                  pl.BlockSpec((k, tn), lambda i, j: (1, j))],
        out_specs=pl.BlockSpec((tm, tn), lambda i, j: (i, j)),
        out_shape=jax.ShapeDtypeStruct((m, n), F32),
        compiler_params=_params("parallel", "arbitrary"),
        name=name,
    )(x1, x2, w, w)


def _softplus(z):
    return jnp.maximum(z, 0.0) + jnp.log(1.0 + jnp.exp(-jnp.abs(z)))


def _head_sums(x, head_dim):
    shift = head_dim.bit_length() - 1
    assert head_dim == 1 << shift
    r = lax.shift_right_logical(lax.broadcasted_iota(jnp.int32, (LANES, LANES), 0), shift)
    c = lax.shift_right_logical(lax.broadcasted_iota(jnp.int32, (LANES, LANES), 1), shift)
    ones_bd = jnp.where(r == c, 1.0, 0.0).astype(BF16)
    hi = x.astype(BF16)
    lo = (x - hi.astype(F32)).astype(BF16)
    out = []
    for j in range(x.shape[1] // LANES):
        sl = slice(j * LANES, (j + 1) * LANES)
        out.append(_bdot(hi[:, sl], ones_bd) + _bdot(lo[:, sl], ones_bd))
    return jnp.concatenate(out, axis=1)


def _token_shift(y, prev_row, next_row, mu):
    n = y.shape[0]
    rows = lax.broadcasted_iota(jnp.int32, y.shape, 0)
    up = jnp.where(rows == 0, prev_row, pltpu.roll(y, 1, axis=0))
    down = jnp.where(rows == n - 1, next_row, pltpu.roll(y, n - 1, axis=0))
    return y + mu * (0.5 * (up + down) - y)


def _prep_body(y_ref, yp_ref, yn_ref, l_ref, lp_ref, ln_ref, mu_ref, mul_ref, kk_g_ref, ka_ref, w0_ref, a0_ref,
               wup_ref, aup_ref, gup_ref, r_ref, v_ref, kk_ref, g_ref, lw_ref, kd_ref, ad_ref, *, n_ctx, n_all):
    tt = y_ref.shape[1]
    d = r_ref.shape[2]
    t0 = pl.program_id(1) * tt
    first = (t0 == 0) | (t0 == n_ctx)
    final = (t0 + tt == n_ctx) | (t0 + tt == n_all)
    edge = lambda ref, row, off: jnp.where(off, 0.0, ref[0, row:row + 1, :])
    ys = _token_shift(y_ref[0], edge(yp_ref, 7, first), edge(yn_ref, 0, final), mu_ref[...])
    yl = _token_shift(l_ref[0], edge(lp_ref, 7, first), edge(ln_ref, 0, final), mul_ref[...])
    r_ref[0] = ys[:, :d]
    k = ys[:, d:2 * d]
    v_ref[0] = ys[:, 2 * d:]
    kx = k * kk_g_ref[...]
    kk_ref[0] = kx * lax.rsqrt(_head_sums(kx * kx, RWKV_HEAD_DIM) + 1e-12)
    rw = wup_ref.shape[0]
    ra = aup_ref.shape[0]
    xw, xa, xg = yl[:, :rw], yl[:, rw:rw + ra], yl[:, rw + ra:]
    g_ref[0] = _dot(jax.nn.sigmoid(xg), gup_ref[...])
    w_log = -_softplus(-(w0_ref[...] + _dot(jnp.tanh(xw), wup_ref[...]))) - 0.5
    lw_ref[0] = -jnp.exp(w_log)
    a = jax.nn.sigmoid(a0_ref[...] + _dot(xa, aup_ref[...]))
    ad_ref[0] = a
    kd_ref[0] = jnp.concatenate([k, k], axis=1) * (1.0 + (a - 1.0) * ka_ref[...])


def rwkv_prep(y, lora_block, mu_rkv, mu_lora, k_k, k_a, w0, a0, w_up, a_up, g_up, *, n_ctx, name):
    b, t, _ = y.shape
    d = k_k.shape[0]
    wl = mu_lora.shape[0]
    tt = _row_tile(n_ctx, 256)
    assert (t - n_ctx) % tt == 0 and tt % 8 == 0
    nb8 = tt // 8
    cur = lambda col: (lambda i, j: (i, j, col))
    prev = lambda col: (lambda i, j: (i, jnp.maximum(j * nb8 - 1, 0), col))
    nxt = lambda col: (lambda i, j: (i, jnp.minimum((j + 1) * nb8, t // 8 - 1), col))
    const = lambda i, j: (0, 0)
    two = lambda z: jnp.concatenate([z[0], z[1]], axis=-1)
    pad_rows = lambda z: jnp.pad(z, ((0, 0), (0, -(-z.shape[1] // LANES) * LANES - z.shape[1]), (0, 0)))
    wup2 = two(pad_rows(w_up))
    aup2 = two(pad_rows(a_up))
    one = jax.ShapeDtypeStruct((b, t, d), F32)
    both = jax.ShapeDtypeStruct((b, t, 2 * d), F32)
    o1 = pl.BlockSpec((1, tt, d), lambda i, j: (i, j, 0))
    o2 = pl.BlockSpec((1, tt, 2 * d), lambda i, j: (i, j, 0))
    return pl.pallas_call(
        functools.partial(_prep_body, n_ctx=n_ctx, n_all=t),
        grid=(b, t // tt),
        in_specs=[pl.BlockSpec((1, tt, 3 * d), cur(0)),
                  pl.BlockSpec((1, 8, 3 * d), prev(0)),
                  pl.BlockSpec((1, 8, 3 * d), nxt(0)),
                  pl.BlockSpec((1, tt, wl), cur(lora_block)),
                  pl.BlockSpec((1, 8, wl), prev(lora_block)),
                  pl.BlockSpec((1, 8, wl), nxt(lora_block)),
                  pl.BlockSpec((1, 3 * d), const), pl.BlockSpec((1, wl), const),
                  pl.BlockSpec((1, d), const), pl.BlockSpec((1, 2 * d), const),
                  pl.BlockSpec((1, 2 * d), const), pl.BlockSpec((1, 2 * d), const),
                  pl.BlockSpec(wup2.shape, const), pl.BlockSpec(aup2.shape, const),
                  pl.BlockSpec(g_up.shape, const)],
        out_specs=[o1, o1, o1, o1, o2, o2, o2],
        out_shape=[one, one, one, one, both, both, both],
        compiler_params=_params("parallel", "parallel"),
        name=name,
    )(y, y, y, y, y, y, mu_rkv[None, :], mu_lora[None, :], k_k[None, :], jnp.tile(k_a, 2)[None, :],
      two(w0)[None, :], two(a0)[None, :], wup2, aup2, g_up)


def _post_body(yf_ref, yb_ref, r_ref, kd_ref, v_ref, g_ref, rk_ref, lg_ref, lb_ref, o_ref):
    d = o_ref.shape[2]
    inv = 1.0 / RWKV_HEAD_DIM
    wk = yf_ref[0] + yb_ref[0]
    cen = wk - _head_sums(wk, RWKV_HEAD_DIM) * inv
    var = _head_sums(cen * cen, RWKV_HEAD_DIM) * inv
    gn = cen * lax.rsqrt(var + RWKV_GN_EPS) * lg_ref[...] + lb_ref[...]
    ksum = kd_ref[0, :, :d] + kd_ref[0, :, d:]
    bonus = _head_sums(r_ref[0] * ksum * rk_ref[...], RWKV_HEAD_DIM) * v_ref[0]
    o_ref[0] = (gn + bonus) * g_ref[0]


def rwkv_post(y_f, y_b, r, kd2, v, g, r_k, lnx_g, lnx_b, name):
    b, t, d = r.shape
    tt = _row_tile(t, 256)
    blk = pl.BlockSpec((1, tt, d), lambda i, j: (i, j, 0))
    vec = pl.BlockSpec((1, d), lambda i, j: (0, 0))
    return pl.pallas_call(
        _post_body,
        grid=(b, t // tt),
        in_specs=[blk, blk, blk, pl.BlockSpec((1, tt, 2 * d), lambda i, j: (i, j, 0)), blk, blk, vec, vec, vec],
        out_specs=blk,
        out_shape=jax.ShapeDtypeStruct((b, t, d), F32),
        compiler_params=_params("parallel", "parallel"),
        name=name,
    )(y_f, y_b, r, kd2, v, g, r_k.reshape(1, d), lnx_g[None, :], lnx_b[None, :])


def _cumsum_rows(tri_bf16, x):
    hi = x.astype(BF16)
    r1 = x - hi.astype(F32)
    mid = r1.astype(BF16)
    lo = (r1 - mid.astype(F32)).astype(BF16)
    parts = jnp.concatenate([hi, mid, lo], axis=1)
    acc = jnp.dot(tri_bf16, parts, preferred_element_type=F32)
    n = x.shape[1]
    return acc[:, :n] + acc[:, n:2 * n] + acc[:, 2 * n:]


def _bdot(a, b):
    return jnp.dot(a, b, preferred_element_type=F32)


def _wkv_body(r_ref, v_ref, kk_ref, lw_ref, k_ref, a_ref, y_ref, s_ref, *, reverse, n_pairs):
    c = WKV_CHUNK
    w2 = 2 * c

    @pl.when(pl.program_id(1) == 0)
    def _():
        s_ref[...] = jnp.zeros_like(s_ref)

    row = lax.broadcasted_iota(jnp.int32, (w2, w2), 0)
    col = lax.broadcasted_iota(jnp.int32, (w2, w2), 1)
    row_a = row < c
    same = row_a == (col < c)
    tt = jnp.where(row_a, row, row - c)
    ss = jnp.where(col < c, col, col - c)
    before = (ss > tt) if reverse else (ss < tt)
    strict = same & before
    incl = same & (before | (ss == tt))
    r64 = lax.broadcasted_iota(jnp.int32, (c, c), 0)
    c64 = lax.broadcasted_iota(jnp.int32, (c, c), 1)
    tri = ((c64 >= r64) if reverse else (c64 <= r64)).astype(BF16)
    lane_a = lax.broadcasted_iota(jnp.int32, (c, w2), 1) < c
    last = 0 if reverse else c - 1
    pairs = range(n_pairs)
    sls = [slice(p * w2, (p + 1) * w2) for p in pairs]
    zero = jnp.zeros((c, w2), F32)
    zero2 = jnp.zeros((w2, w2), F32)

    def split_heads(x):
        return jnp.concatenate([jnp.where(lane_a, x, zero), jnp.where(lane_a, zero, x)], axis=0)

    lw = [lw_ref[0, :, sl] for sl in sls]
    big_l = [_cumsum_rows(tri, lw[p]) for p in pairs]
    l_tot = [big_l[p][last:last + 1, :] for p in pairs]

    kk = [kk_ref[0, :, sl] for sl in sls]
    kd = [k_ref[0, :, sl] for sl in sls]
    bb = [kk[p] * a_ref[0, :, sls[p]] for p in pairs]
    vv = [v_ref[0, :, sl] for sl in sls]
    e_neg = [jnp.exp(-big_l[p]) for p in pairs]
    q_t = [kk[p] * jnp.exp(big_l[p] - lw[p]) for p in pairs]
    r_t = [r_ref[0, :, sls[p]] * jnp.exp(big_l[p]) for p in pairs]
    k_t = [(kd[p] * e_neg[p]).astype(BF16) for p in pairs]
    b_t = [(bb[p] * e_neg[p]).astype(BF16) for p in pairs]
    e_rest = [jnp.exp(l_tot[p] - big_l[p]) for p in pairs]
    kb_p = [jnp.concatenate([kd[p] * e_rest[p], bb[p] * e_rest[p]], axis=0).astype(BF16) for p in pairs]

    lhs4 = [jnp.concatenate([split_heads(q_t[p]), split_heads(r_t[p])], axis=0).astype(BF16) for p in pairs]
    s_bd = [s_ref[p] for p in pairs]
    big = [lax.dot_general(lhs4[p],
                           jnp.concatenate([b_t[p], k_t[p], k_t[p], b_t[p], s_bd[p].astype(BF16)], axis=0),
                           (((1,), (1,)), ((), ())), preferred_element_type=F32) for p in pairs]
    n_bd, m_both, nr_bd = [], [], []
    for p in pairs:
        q1, r1 = big[p][:w2, :w2], big[p][w2:, :w2]
        q2, r2 = big[p][:w2, w2:2 * w2], big[p][w2:, w2:2 * w2]
        n_bd.append(jnp.where(strict, jnp.where(row_a, q1, q2), zero2).astype(BF16))
        nr_bd.append(jnp.where(incl, jnp.where(row_a, r1, r2), zero2).astype(BF16))
        m_both.append(jnp.concatenate([jnp.where(strict, jnp.where(row_a, q2, q1), zero2),
                                       jnp.where(incl, jnp.where(row_a, r2, r1), zero2)], axis=0).astype(BF16))
    mv = [_bdot(m_both[p], split_heads(vv[p]).astype(BF16)) for p in pairs]

    x = [big[p][:w2, 2 * w2:] + mv[p][:w2] for p in pairs]
    pw = n_bd
    span = 1
    while span < c:
        if 2 * span < c:
            both = [_bdot(pw[p], jnp.concatenate([x[p].astype(BF16), pw[p]], axis=1)) for p in pairs]
            pw = [both[p][:, w2:].astype(BF16) for p in pairs]
        else:
            both = [_bdot(pw[p], x[p].astype(BF16)) for p in pairs]
        x = [(x[p] - both[p][:, :w2]) if span == 1 else (x[p] + both[p][:, :w2]) for p in pairs]
        span *= 2
    u_bs = x

    nru = [_bdot(nr_bd[p], u_bs[p].astype(BF16)) for p in pairs]
    for p in pairs:
        y_bs = big[p][w2:, 2 * w2:] + mv[p][w2:] - nru[p]
        y_ref[0, :, sls[p]] = y_bs[:c] + y_bs[c:]

    vu = [jnp.concatenate([vv[p], -(u_bs[p][:c] + u_bs[p][c:])], axis=0).astype(BF16) for p in pairs]
    upd = [lax.dot_general(vu[p], kb_p[p], (((0,), (0,)), ((), ())), preferred_element_type=F32)
           for p in pairs]
    for p in pairs:
        s_ref[p] = s_bd[p] * jnp.exp(l_tot[p]) + jnp.where(same, upd[p], zero2)


def wkv_scan(r, v, kk, lw, kd, ad, *, n_ctx_chunks, reverse, name):
    b, t, d = r.shape
    c = WKV_CHUNK
    nc = t // c
    lc = n_ctx_chunks
    n_pairs = d // (2 * RWKV_HEAD_DIM)

    if reverse:
        def pos(s):
            return jnp.where(s < lc, lc - 1 - s, nc - 1 - s + lc)
    else:
        def pos(s):
            return s
    spec = pl.BlockSpec((1, c, d), lambda i, s: (i, pos(s), 0))
    side = pl.BlockSpec((1, c, d), lambda i, s: (i, pos(s), 1 if reverse else 0))
    return pl.pallas_call(
        functools.partial(_wkv_body, reverse=reverse, n_pairs=n_pairs),
        grid=(b, nc),
        in_specs=[spec, spec, spec, side, side, side],
        out_specs=spec,
        out_shape=jax.ShapeDtypeStruct((b, t, d), F32),
        scratch_shapes=[pltpu.VMEM((n_pairs, 2 * c, 2 * c), F32)],
        compiler_params=_params("parallel", "arbitrary"),
        name=name,
    )(r, v, kk, lw, kd, ad)


def _gelu(x):
    return 0.5 * x * (1.0 + lax.erf(x * (2.0 ** -0.5)))


def _gmlp_body(u_ref, v_ref, g_ref, b_ref, ws_ref, bs_ref, o_ref, *, groups):
    u = _gelu(u_ref[0])
    v = _gelu(v_ref[0])
    mu = jnp.mean(v, axis=-1, keepdims=True)
    var = jnp.mean(jnp.square(v - mu), axis=-1, keepdims=True)
    vn = (v - mu) * lax.rsqrt(var + GMLP_LN_EPS) * g_ref[...] + b_ref[...]
    gd = vn.shape[1] // groups
    for g in range(groups):
        sl = slice(g * gd, (g + 1) * gd)
        mixed = _dot(ws_ref[g], vn[:, sl]) + bs_ref[g]
        o_ref[0, :, sl] = u[:, sl] * mixed


def gmlp(y, u_block, v_block, ln_g, ln_b, ws, bs, name):
    b, t, _ = y.shape
    groups, chunk, _ = ws.shape
    dg = ln_g.shape[0]
    gd = dg // groups
    bs_b = jnp.broadcast_to(bs[:, :, None], (groups, chunk, gd))
    return pl.pallas_call(
        functools.partial(_gmlp_body, groups=groups),
        grid=(b, t // chunk),
        in_specs=[pl.BlockSpec((1, chunk, dg), lambda i, n: (i, n, u_block)),
                  pl.BlockSpec((1, chunk, dg), lambda i, n: (i, n, v_block)),
                  pl.BlockSpec((1, dg), lambda i, n: (0, 0)),
                  pl.BlockSpec((1, dg), lambda i, n: (0, 0)),
                  pl.BlockSpec((groups, chunk, chunk), lambda i, n: (0, 0, 0)),
                  pl.BlockSpec((groups, chunk, gd), lambda i, n: (0, 0, 0))],
        out_specs=pl.BlockSpec((1, chunk, dg), lambda i, n: (i, n, 0)),
        out_shape=jax.ShapeDtypeStruct((b, t, dg), F32),
        compiler_params=_params("parallel", "parallel"),
        name=name,
    )(y, y, ln_g[None, :], ln_b[None, :], ws, bs_b)


def _norm_rope(x, gain, cos, sin):
    x = x * lax.rsqrt(jnp.mean(x * x, axis=-1, keepdims=True) + NORM_EPS) * gain
    lane = lax.broadcasted_iota(jnp.int32, x.shape, 1)
    quarter = ATT_HEAD_DIM // 4
    partner = jnp.where((lane & quarter) == 0, pltpu.roll(x, ATT_HEAD_DIM - quarter, axis=1),
                        pltpu.roll(x, quarter, axis=1))
    return x * cos + partner * sin


def _kprep_body(k_ref, g_ref, cos_ref, sin_ref, o_ref):
    hd = ATT_HEAD_DIM
    for h in range(k_ref.shape[2] // hd):
        sl = slice(h * hd, (h + 1) * hd)
        o_ref[0, :, sl] = _norm_rope(k_ref[0, :, sl], g_ref[...], cos_ref[...], sin_ref[...]).astype(BF16)


def key_prep(y, k_block, k_g, cos, sin, n_kv, name):
    b, t, _ = y.shape
    w = n_kv * ATT_HEAD_DIM
    tt = _row_tile(t, 256)
    tab = pl.BlockSpec((tt, ATT_HEAD_DIM), lambda i, j: (j, 0))
    return pl.pallas_call(
        _kprep_body,
        grid=(b, t // tt),
        in_specs=[pl.BlockSpec((1, tt, w), lambda i, j: (i, j, k_block)),
                  pl.BlockSpec((1, ATT_HEAD_DIM), lambda i, j: (0, 0)), tab, tab],
        out_specs=pl.BlockSpec((1, tt, w), lambda i, j: (i, j, 0)),
        out_shape=jax.ShapeDtypeStruct((b, t, w), BF16),
        compiler_params=_params("parallel", "parallel"),
        name=name,
    )(y, k_g[None, :], cos, sin)


def _attn_body(q_ref, k_ref, v_ref, g_ref, cos_ref, sin_ref, o_ref, *, group):
    k = k_ref[0]
    v = v_ref[0].astype(BF16)
    hd = ATT_HEAD_DIM
    scale = hd ** -0.5
    for g in range(group):
        sl = slice(g * hd, (g + 1) * hd)
        q = _norm_rope(q_ref[0, :, sl], g_ref[...], cos_ref[...], sin_ref[...]) * scale
        s = _dot_nt(q, k)
        p = jnp.exp(s - jnp.max(s, axis=-1, keepdims=True))
        denom = jnp.sum(p, axis=-1, keepdims=True)
        o_ref[0, :, sl] = jnp.dot(p.astype(BF16), v, preferred_element_type=F32) / denom


def attention(y, kb, q_g, cos, sin, *, n_q, n_kv, n_ctx, name):
    b, t, _ = y.shape
    hd = ATT_HEAD_DIM
    group = n_q // n_kv
    tq = t - n_ctx
    bq = _row_tile(n_ctx, 256)
    assert tq % bq == 0
    off = n_ctx // bq
    tab = pl.BlockSpec((bq, hd), lambda i, h, j: (off + j, 0))
    return pl.pallas_call(
        functools.partial(_attn_body, group=group),
        grid=(b, n_kv, tq // bq),
        in_specs=[pl.BlockSpec((1, bq, group * hd), lambda i, h, j: (i, off + j, h)),
                  pl.BlockSpec((1, t, hd), lambda i, h, j: (i, 0, h)),
                  pl.BlockSpec((1, t, hd), lambda i, h, j: (i, 0, n_q + n_kv + h)),
                  pl.BlockSpec((1, hd), lambda i, h, j: (0, 0)), tab, tab],
        out_specs=pl.BlockSpec((1, bq, group * hd), lambda i, h, j: (i, j, h)),
        out_shape=jax.ShapeDtypeStruct((b, tq, n_q * hd), F32),
        compiler_params=_params("parallel", "parallel", "arbitrary"),
        name=name,
    )(y, kb, y, q_g[None, :], cos, sin)


def _router_body(h_ref, w_ref, b_ref, idx_ref, wt_ref, pos_ref, cnt_ref, run_ref):
    logits = jnp.dot(h_ref[...], w_ref[...], precision=lax.Precision.HIGHEST,
                     preferred_element_type=F32) + b_ref[...]
    n_e = logits.shape[1]
    lane = lax.broadcasted_iota(jnp.int32, logits.shape, 1)
    vals, ids = [], []
    cur = logits
    for _ in range(TOP_K):
        m = jnp.max(cur, axis=1, keepdims=True)
        sel = jnp.min(jnp.where(cur == m, lane, n_e), axis=1, keepdims=True)
        vals.append(m)
        ids.append(sel)
        cur = jnp.where(lane == sel, -jnp.inf, cur)
    e = jnp.exp(jnp.concatenate(vals, axis=1) - vals[0])
    idx_ref[...] = jnp.concatenate(ids, axis=1)
    wt_ref[...] = e / jnp.sum(e, axis=1, keepdims=True)

    @pl.when(pl.program_id(0) == 0)
    def _():
        run_ref[...] = jnp.zeros_like(run_ref)

    tm = logits.shape[0]
    hot = [jnp.where(lane == sel, 1.0, 0.0) for sel in ids]
    chosen = hot[0] + hot[1] + hot[2] + hot[3]
    tri = jnp.where(lax.broadcasted_iota(jnp.int32, (tm, tm), 1) < lax.broadcasted_iota(jnp.int32, (tm, tm), 0),
                    1.0, 0.0).astype(BF16)
    base = run_ref[...] + jnp.dot(tri, chosen.astype(BF16), preferred_element_type=F32)
    pos_ref[...] = jnp.concatenate([jnp.sum(h * base, axis=1, keepdims=True) for h in hot],
                                   axis=1).astype(jnp.int32)
    run_ref[...] += jnp.sum(chosen, axis=0, keepdims=True)
    cnt_ref[...] = run_ref[...]


def router(h, rw, rb, name):
    t, d = h.shape
    n_e = rw.shape[1]
    tm = _row_tile(t, 256)
    return pl.pallas_call(
        _router_body,
        grid=(t // tm,),
        in_specs=[pl.BlockSpec((tm, d), lambda i: (i, 0)),
                  pl.BlockSpec((d, n_e), lambda i: (0, 0)),
                  pl.BlockSpec((1, n_e), lambda i: (0, 0))],
        out_specs=[pl.BlockSpec((tm, TOP_K), lambda i: (i, 0)),
                   pl.BlockSpec((tm, TOP_K), lambda i: (i, 0)),
                   pl.BlockSpec((tm, TOP_K), lambda i: (i, 0)),
                   pl.BlockSpec((1, n_e), lambda i: (0, 0))],
        out_shape=[jax.ShapeDtypeStruct((t, TOP_K), jnp.int32),
                   jax.ShapeDtypeStruct((t, TOP_K), F32),
                   jax.ShapeDtypeStruct((t, TOP_K), jnp.int32),
                   jax.ShapeDtypeStruct((1, n_e), F32)],
        scratch_shapes=[pltpu.VMEM((1, n_e), F32)],
        compiler_params=_params("arbitrary"),
        name=name,
    )(h, rw, rb[None, :])


def _dispatch_body(dest_ref, x_ref, init_ref, xs_ref, sem, *, rows):
    del init_ref

    def row_copy(r, j):
        return pltpu.make_async_copy(x_ref.at[pl.ds(r, 1)], xs_ref.at[pl.ds(dest_ref[0, 0, r * TOP_K + j], 1)], sem)

    def issue(r, carry):
        for j in range(TOP_K):
            row_copy(r, j).start()
        return carry

    def drain(r, carry):
        for j in range(TOP_K):
            row_copy(r, j).wait()
        return carry

    lax.fori_loop(0, rows, issue, 0, unroll=8)
    lax.fori_loop(0, rows, drain, 0, unroll=8)


def dispatch(tokens, dest, np_rows, name):
    t, d = tokens.shape
    rows = _row_tile(t, 256)
    n_tiles = t // rows
    return pl.pallas_call(
        functools.partial(_dispatch_body, rows=rows),
        grid=(n_tiles,),
        in_specs=[pl.BlockSpec((1, 1, rows * TOP_K), lambda i: (i, 0, 0), memory_space=pltpu.SMEM),
                  pl.BlockSpec((rows, d), lambda i: (i, 0)),
                  pl.BlockSpec(memory_space=pl.ANY)],
        out_specs=pl.BlockSpec(memory_space=pl.ANY),
        out_shape=jax.ShapeDtypeStruct((np_rows, d), F32),
        scratch_shapes=[pltpu.SemaphoreType.DMA],
        input_output_aliases={2: 0},
        compiler_params=_params("arbitrary"),
        name=name,
    )(dest.reshape(n_tiles, 1, rows * TOP_K), tokens, jnp.zeros((np_rows, d), F32))


def _ffn_body(te_ref, nv_ref, x_ref, wg_ref, wl_ref, bg_ref, bl_ref, w2_ref, b2_ref, o_ref, xb_ref):
    i = pl.program_id(0)
    c = pl.program_id(1)
    valid = i < nv_ref[0]

    @pl.when(valid & (c == 0))
    def _():
        xb_ref[...] = x_ref[...].astype(BF16)
        o_ref[...] = jnp.broadcast_to(b2_ref[0], o_ref.shape)

    @pl.when(valid)
    def _():
        x = xb_ref[...]
        glu = _dot(x, wg_ref[0]) + bg_ref[0]
        lin = _dot(x, wl_ref[0]) + bl_ref[0]
        glu = jnp.minimum(glu, SWIGLU_LIMIT)
        lin = jnp.clip(lin, -SWIGLU_LIMIT, SWIGLU_LIMIT)
        act = glu * jax.nn.sigmoid(SWIGLU_ALPHA * glu) * (lin + 1.0)
        o_ref[...] += _dot(act, w2_ref[0])

    @pl.when(jnp.logical_not(valid) & (c == 0))
    def _():
        o_ref[...] = jnp.zeros_like(o_ref)


def moe_ffn(xs, tile_expert, n_valid, w1, b1, w2, b2, *, layer, tm, fc, name):
    np_rows, d = xs.shape
    f = w2.shape[2]
    nfc = f // fc
    n_tiles = np_rows // tm
    grid_spec = pltpu.PrefetchScalarGridSpec(
        num_scalar_prefetch=2,
        grid=(n_tiles, nfc),
        in_specs=[pl.BlockSpec((tm, d), lambda i, c, te, nv: (i, 0)),
                  pl.BlockSpec((None, 1, d, fc), lambda i, c, te, nv: (layer, te[i], 0, c)),
                  pl.BlockSpec((None, 1, d, fc), lambda i, c, te, nv: (layer, te[i], 0, nfc + c)),
                  pl.BlockSpec((1, 1, fc), lambda i, c, te, nv: (te[i], 0, c)),
                  pl.BlockSpec((1, 1, fc), lambda i, c, te, nv: (te[i], 0, nfc + c)),
                  pl.BlockSpec((None, 1, fc, d), lambda i, c, te, nv: (layer, te[i], c, 0)),
                  pl.BlockSpec((1, 1, d), lambda i, c, te, nv: (te[i], 0, 0))],
        out_specs=pl.BlockSpec((tm, d), lambda i, c, te, nv: (i, 0)),
        scratch_shapes=[pltpu.VMEM((tm, d), BF16)],
    )
    return pl.pallas_call(
        _ffn_body,
        grid_spec=grid_spec,
        out_shape=jax.ShapeDtypeStruct((np_rows, d), F32),
        compiler_params=_params("arbitrary", "arbitrary"),
        name=name,
    )(tile_expert, n_valid, xs, w1, w1, b1, b1, w2, b2)


def moe(tokens, rw, rb, w1, b1, w2, b2, layer, tag):
    t, d = tokens.shape
    n_e = rw.shape[1]
    tm = 1024 if t * TOP_K >= 1024 * n_e else 128
    fc = 256
    idx, wts, pos, cnt = router(tokens, rw, rb, "router" + tag)

    n_pairs = t * TOP_K
    counts = cnt[0].astype(jnp.int32)
    padded = (counts + tm - 1) // tm * tm
    ends_p = jnp.cumsum(padded)
    starts_p = ends_p - padded
    e_ids = jnp.arange(n_e, dtype=jnp.int32)
    dest = pos + jnp.sum(jnp.where(idx[:, :, None] == e_ids, starts_p, 0), axis=-1)
    n_tiles = -(-n_pairs // tm) + n_e
    np_rows = n_tiles * tm
    n_valid = (ends_p[-1] // tm).astype(jnp.int32)
    tile_start = jnp.arange(n_tiles, dtype=jnp.int32) * tm
    tile_expert = jnp.sum((tile_start[:, None] >= ends_p[None, :]).astype(jnp.int32), axis=1)
    last_e = jnp.sum((ends_p[-1] - 1 >= ends_p).astype(jnp.int32))
    tile_expert = jnp.minimum(tile_expert, last_e)

    xs = dispatch(tokens, dest, np_rows, "dispatch" + tag)
    ys = moe_ffn(xs, tile_expert, n_valid[None], w1, b1[:, None, :], w2, b2[:, None, :], layer=layer,
                 tm=tm, fc=fc, name="moe_ffn" + tag)
    out = ys[dest[:, 0]] * wts[:, 0:1]
    for k in range(1, TOP_K):
        out = out + ys[dest[:, k]] * wts[:, k:k + 1]
    return out


def _rmsnorm(x, g):
    return x * lax.rsqrt(jnp.mean(x * x, axis=-1, keepdims=True) + NORM_EPS) * g


def _adaln(cvec, w_mod, b_mod, name):
    m = project(jax.nn.silu(cvec), w_mod, name) + b_mod
    return jnp.split(m, N_MOD, axis=-1)


def _even_layout(d_rwkv, lora_w, lora_a, lora_g, d_gmlp):
    pad = lambda n: -(-n // LANES) * LANES
    d_shift = 3 * d_rwkv + lora_w + lora_a + lora_g
    src = {"rkv": (0, 3 * d_rwkv), "xw": (3 * d_rwkv, lora_w), "xa": (3 * d_rwkv + lora_w, lora_a),
           "xg": (3 * d_rwkv + lora_w + lora_a, lora_g), "gmlp": (d_shift, 2 * d_gmlp)}
    order = ["rkv", "gmlp", "xw", "xa", "xg"]
    dst, off = {}, 0
    for name in order:
        dst[name] = off
        off += pad(src[name][1])
    return src, dst, order, off


def _regroup_cols(w, src, dst, order, total):
    parts = []
    for name in order:
        s, n = src[name]
        blk = w[..., s:s + n]
        padn = -(-n // LANES) * LANES - n
        if padn:
            blk = jnp.pad(blk, [(0, 0)] * (w.ndim - 1) + [(0, padn)])
        parts.append(blk)
    return jnp.concatenate(parts, axis=-1)


def _even_mixer(nx, nc, p, need_ctx, tag):
    (w_in, mu, w0, w_up, a0, a_up, g_up, k_k, k_a, r_k, lnx_g, lnx_b, v_ln_g, v_ln_b, ws, bs, w_out) = p
    b, s, d = nx.shape
    l = nc.shape[1]
    d_rwkv = g_up.shape[1]
    lora_w, lora_a, lora_g = w_up.shape[1], a_up.shape[1], g_up.shape[0]
    d_gmlp = v_ln_g.shape[0]
    src, dst, order, total = _even_layout(d_rwkv, lora_w, lora_a, lora_g, d_gmlp)
    w_in_r = _regroup_cols(w_in, src, dst, order, total)
    mu_r = _regroup_cols(jnp.concatenate([mu, jnp.zeros((2 * d_gmlp,), F32)]), src, dst, order, total)

    t = l + s
    n_all = jnp.concatenate([nc, nx], axis=1)
    y = project(n_all.reshape(b * t, d), w_in_r, "even_in" + tag).reshape(b, t, total)

    wl = total - dst["xw"]
    assert dst["xw"] % wl == 0 and dst["gmlp"] % d_gmlp == 0
    r, v, kk, g, lw2, kd2, ad2 = rwkv_prep(y, dst["xw"] // wl, mu_r[:3 * d_rwkv], mu_r[dst["xw"]:], k_k, k_a,
                                           w0, a0, w_up, a_up, g_up, n_ctx=l, name="rwkv_prep" + tag)
    y_dir = [wkv_scan(r, v, kk, lw2, kd2, ad2, n_ctx_chunks=l // WKV_CHUNK, reverse=dd == 1,
                      name="wkv%d%s" % (dd, tag)) for dd in range(2)]
    rw_out = rwkv_post(y_dir[0], y_dir[1], r, kd2, v, g, r_k, lnx_g, lnx_b, "rwkv_post" + tag)

    ub = dst["gmlp"] // d_gmlp
    g_out = gmlp(y, ub, ub + 1, v_ln_g, v_ln_b, ws, bs, "gmlp" + tag)
    mix = matmul2(rw_out.reshape(b * t, d_rwkv), g_out.reshape(b * t, d_gmlp), w_out,
                  "even_out" + tag).reshape(b, t, d)
    return mix[:, l:], (mix[:, :l] if need_ctx else None)


def _rope_tables(n_ctx, rows):
    row = jnp.repeat(jnp.arange(rows, dtype=F32), GRID_W)
    col = jnp.tile(jnp.arange(GRID_W, dtype=F32), rows)
    n_freq = ATT_HEAD_DIM // 4
    inv = jnp.power(jnp.float32(ROPE_THETA), -jnp.arange(n_freq, dtype=F32) / n_freq)
    ang_r = row[:, None] * inv
    ang_c = col[:, None] * inv
    cos = jnp.concatenate([jnp.cos(ang_r), jnp.cos(ang_r), jnp.cos(ang_c), jnp.cos(ang_c)], axis=1)
    sin = jnp.concatenate([-jnp.sin(ang_r), jnp.sin(ang_r), -jnp.sin(ang_c), jnp.sin(ang_c)], axis=1)
    cos = jnp.concatenate([jnp.ones((n_ctx, ATT_HEAD_DIM), F32), cos], axis=0)
    sin = jnp.concatenate([jnp.zeros((n_ctx, ATT_HEAD_DIM), F32), sin], axis=0)
    return cos, sin


def _odd_mixer(nx, nc, p, rope, need_ctx, tag):
    w_in, q_g, k_g, w_out = p
    b, s, d = nx.shape
    l = nc.shape[1]
    hd = ATT_HEAD_DIM
    n_q = w_out.shape[0] // hd
    n_kv = (w_in.shape[1] // hd - n_q) // 2
    assert not need_ctx, "context-query attention is only needed when an attention layer is not the last layer"
    assert n_q % n_kv == 0
    cos, sin = rope
    t = l + s
    n_all = jnp.concatenate([nc, nx], axis=1)
    y = project(n_all.reshape(b * t, d), w_in, "odd_in" + tag).reshape(b, t, -1)
    kb = key_prep(y, n_q // n_kv, k_g, cos, sin, n_kv, "key_prep" + tag)
    ox = attention(y, kb, q_g, cos, sin, n_q=n_q, n_kv=n_kv, n_ctx=l, name="attn" + tag)
    mix_x = project(ox.reshape(b * s, -1), w_out, "odd_out" + tag).reshape(b, s, d)
    return mix_x, None


def kernel(x, c, ctx, c_ctx, norm1_g, norm2_g, w_mod, b_mod, e_w_in, e_mu, e_w0, e_w_up, e_a0, e_a_up, e_g_up, e_k_k, e_k_a, e_r_k, e_lnx_g, e_lnx_b, e_v_ln_g, e_v_ln_b, e_ws, e_bs, e_w_out, o_w_in, o_q_g, o_k_g, o_w_out, router_w, router_b, moe_w1, moe_b1, moe_w2, moe_b2, final_g):
    b, s, d = x.shape
    l = ctx.shape[1]
    depth = norm1_g.shape[0]
    rope = _rope_tables(l, s // GRID_W)
    h_lat, h_ctx = x, ctx
    for i in range(depth):
        tag = "_l%d" % i
        need_ctx = i < depth - 1
        j = i // 2
        mods = _adaln(jnp.concatenate([c, c_ctx[None, :]], axis=0), w_mod[i], b_mod[i], "adaln" + tag)
        m_lat = [m[:b, None, :] for m in mods]
        m_ctx = [m[b:, None, :] for m in mods]
        nx = _rmsnorm(h_lat, norm1_g[i]) * (1.0 + m_lat[1]) + m_lat[0]
        nc = _rmsnorm(h_ctx, norm1_g[i]) * (1.0 + m_ctx[1]) + m_ctx[0]
        if i % 2 == 0:
            p = (e_w_in[j], e_mu[j], e_w0[j], e_w_up[j], e_a0[j], e_a_up[j], e_g_up[j], e_k_k[j], e_k_a[j],
                 e_r_k[j], e_lnx_g[j], e_lnx_b[j], e_v_ln_g[j], e_v_ln_b[j], e_ws[j], e_bs[j], e_w_out[j])
            mx, mc = _even_mixer(nx, nc, p, need_ctx, tag)
        else:
            mx, mc = _odd_mixer(nx, nc, (o_w_in[j], o_q_g[j], o_k_g[j], o_w_out[j]), rope, need_ctx, tag)
        h_lat = h_lat + m_lat[2] * mx
        nx = _rmsnorm(h_lat, norm2_g[i]) * (1.0 + m_lat[4]) + m_lat[3]
        if need_ctx:
            h_ctx = h_ctx + m_ctx[2] * mc
            nc = _rmsnorm(h_ctx, norm2_g[i]) * (1.0 + m_ctx[4]) + m_ctx[3]
            tokens = jnp.concatenate([nx.reshape(b * s, d), nc.reshape(b * l, d)], axis=0)
        else:
            tokens = nx.reshape(b * s, d)
        f = moe(tokens, router_w[i], router_b[i], moe_w1, moe_b1[i], moe_w2, moe_b2[i], i, tag)
        h_lat = h_lat + m_lat[5] * f[:b * s].reshape(b, s, d)
        if need_ctx:
            h_ctx = h_ctx + m_ctx[5] * f[b * s:].reshape(b, l, d)
    return _rmsnorm(h_lat, final_g)
```

```python
import functools

import jax
import jax.numpy as jnp
from jax import lax
from jax.experimental import pallas as pl
from jax.experimental.pallas import tpu as pltpu

F32 = jnp.float32
BF16 = jnp.bfloat16

NORM_EPS = 1e-6
N_MOD = 6
RWKV_HEAD_DIM = 64
RWKV_GN_EPS = 64e-5
GMLP_LN_EPS = 1e-5
ATT_HEAD_DIM = 128
GRID_W = 64
ROPE_THETA = 10000.0
TOP_K = 4
SWIGLU_ALPHA = 1.702
SWIGLU_LIMIT = 7.0

LANES = 128
VMEM_LIMIT_BYTES = 56 * 1024 * 1024

WKV_CHUNK = 64


def _params(*sem):
    return pltpu.CompilerParams(dimension_semantics=sem, vmem_limit_bytes=VMEM_LIMIT_BYTES)


def _dot(a, b):
    return jnp.dot(a.astype(BF16), b.astype(BF16), preferred_element_type=F32)


def _dot_nt(a, b):
    return lax.dot_general(a.astype(BF16), b.astype(BF16), (((1,), (1,)), ((), ())),
                           preferred_element_type=F32)


def _dot_tn(a, b):
    return lax.dot_general(a.astype(BF16), b.astype(BF16), (((0,), (0,)), ((), ())),
                           preferred_element_type=F32)


def _mm_body(x_ref, w_ref, o_ref, xb_ref):
    @pl.when(pl.program_id(1) == 0)
    def _():
        xb_ref[...] = x_ref[...].astype(BF16)

    o_ref[...] = jnp.dot(xb_ref[...], w_ref[...].astype(BF16), preferred_element_type=F32)


def matmul(x, w, *, tm, tn, name):
    m, k = x.shape
    n = w.shape[1]
    assert m % tm == 0 and n % tn == 0, (m, n, tm, tn)
    return pl.pallas_call(
        _mm_body,
        grid=(m // tm, n // tn),
        in_specs=[pl.BlockSpec((tm, k), lambda i, j: (i, 0)),
                  pl.BlockSpec((k, tn), lambda i, j: (0, j))],
        out_specs=pl.BlockSpec((tm, tn), lambda i, j: (i, j)),
        out_shape=jax.ShapeDtypeStruct((m, n), F32),
        scratch_shapes=[pltpu.VMEM((tm, k), BF16)],
        compiler_params=_params("parallel", "arbitrary"),
        name=name,
    )(x, w)


def _row_tile(m, cap):
    t = cap
    while m % t:
        t //= 2
    return t


def project(x2d, w, name):
    m = x2d.shape[0]
    mp = -(-m // 16) * 16
    if mp != m:
        x2d = jnp.pad(x2d, ((0, mp - m), (0, 0)))
    out = matmul(x2d, w, tm=_row_tile(mp, 1024), tn=_row_tile(w.shape[1], 512), name=name)
    return out[:m]


def _mm2_body(x1_ref, x2_ref, w1_ref, w2_ref, o_ref):
    o_ref[...] = _dot(x1_ref[...], w1_ref[...]) + _dot(x2_ref[...], w2_ref[...])


def matmul2(x1, x2, w, name):
    m, k = x1.shape
    n = w.shape[1]
    assert x2.shape == (m, k) and w.shape[0] == 2 * k
    tm, tn = _row_tile(m, 1024), _row_tile(n, 512)
    return pl.pallas_call(
        _mm2_body,
        grid=(m // tm, n // tn),
        in_specs=[pl.BlockSpec((tm, k), lambda i, j: (i, 0)),
                  pl.BlockSpec((tm, k), lambda i, j: (i, 0)),
                  pl.BlockSpec((k, tn), lambda i, j: (0, j)),
                  pl.BlockSpec((k, tn), lambda i, j: (1, j))],
        out_specs=pl.BlockSpec((tm, tn), lambda i, j: (i, j)),
        out_shape=jax.ShapeDtypeStruct((m, n), F32),
        compiler_params=_params("parallel", "arbitrary"),
        name=name,
    )(x1, x2, w, w)


def _softplus(z):
    return jnp.maximum(z, 0.0) + jnp.log(1.0 + jnp.exp(-jnp.abs(z)))


def _head_sums(x, head_dim):
    shift = head_dim.bit_length() - 1
    assert head_dim == 1 << shift
    r = lax.shift_right_logical(lax.broadcasted_iota(jnp.int32, (LANES, LANES), 0), shift)
    c = lax.shift_right_logical(lax.broadcasted_iota(jnp.int32, (LANES, LANES), 1), shift)
    ones_bd = jnp.where(r == c, 1.0, 0.0).astype(BF16)
    hi = x.astype(BF16)
    lo = (x - hi.astype(F32)).astype(BF16)
    out = []
    for j in range(x.shape[1] // LANES):
        sl = slice(j * LANES, (j + 1) * LANES)
        out.append(_bdot(hi[:, sl], ones_bd) + _bdot(lo[:, sl], ones_bd))
    return jnp.concatenate(out, axis=1)


def _token_shift(y, prev_row, next_row, mu):
    n = y.shape[0]
    rows = lax.broadcasted_iota(jnp.int32, y.shape, 0)
    up = jnp.where(rows == 0, prev_row, pltpu.roll(y, 1, axis=0))
    down = jnp.where(rows == n - 1, next_row, pltpu.roll(y, n - 1, axis=0))
    return y + mu * (0.5 * (up + down) - y)


def _prep_body(y_ref, yp_ref, yn_ref, l_ref, lp_ref, ln_ref, mu_ref, mul_ref, kk_g_ref, ka_ref, w0_ref, a0_ref,
               wup_ref, aup_ref, gup_ref, r_ref, v_ref, kk_ref, g_ref, lw_ref, kd_ref, ad_ref, *, n_ctx, n_all):
    tt = y_ref.shape[1]
    d = r_ref.shape[2]
    t0 = pl.program_id(1) * tt
    first = (t0 == 0) | (t0 == n_ctx)
    final = (t0 + tt == n_ctx) | (t0 + tt == n_all)
    edge = lambda ref, row, off: jnp.where(off, 0.0, ref[0, row:row + 1, :])
    ys = _token_shift(y_ref[0], edge(yp_ref, 7, first), edge(yn_ref, 0, final), mu_ref[...])
    yl = _token_shift(l_ref[0], edge(lp_ref, 7, first), edge(ln_ref, 0, final), mul_ref[...])
    r_ref[0] = ys[:, :d]
    k = ys[:, d:2 * d]
    v_ref[0] = ys[:, 2 * d:]
    kx = k * kk_g_ref[...]
    kk_ref[0] = kx * lax.rsqrt(_head_sums(kx * kx, RWKV_HEAD_DIM) + 1e-12)
    rw = wup_ref.shape[0]
    ra = aup_ref.shape[0]
    xw, xa, xg = yl[:, :rw], yl[:, rw:rw + ra], yl[:, rw + ra:]
    g_ref[0] = _dot(jax.nn.sigmoid(xg), gup_ref[...])
    w_log = -_softplus(-(w0_ref[...] + _dot(jnp.tanh(xw), wup_ref[...]))) - 0.5
    lw_ref[0] = -jnp.exp(w_log)
    a = jax.nn.sigmoid(a0_ref[...] + _dot(xa, aup_ref[...]))
    ad_ref[0] = a
    kd_ref[0] = jnp.concatenate([k, k], axis=1) * (1.0 + (a - 1.0) * ka_ref[...])


def rwkv_prep(y, lora_block, mu_rkv, mu_lora, k_k, k_a, w0, a0, w_up, a_up, g_up, *, n_ctx, name):
    b, t, _ = y.shape
    d = k_k.shape[0]
    wl = mu_lora.shape[0]
    tt = _row_tile(n_ctx, 256)
    assert (t - n_ctx) % tt == 0 and tt % 8 == 0
    nb8 = tt // 8
    cur = lambda col: (lambda i, j: (i, j, col))
    prev = lambda col: (lambda i, j: (i, jnp.maximum(j * nb8 - 1, 0), col))
    nxt = lambda col: (lambda i, j: (i, jnp.minimum((j + 1) * nb8, t // 8 - 1), col))
    const = lambda i, j: (0, 0)
    two = lambda z: jnp.concatenate([z[0], z[1]], axis=-1)
    pad_rows = lambda z: jnp.pad(z, ((0, 0), (0, -(-z.shape[1] // LANES) * LANES - z.shape[1]), (0, 0)))
    wup2 = two(pad_rows(w_up))
    aup2 = two(pad_rows(a_up))
    one = jax.ShapeDtypeStruct((b, t, d), F32)
    both = jax.ShapeDtypeStruct((b, t, 2 * d), F32)
    o1 = pl.BlockSpec((1, tt, d), lambda i, j: (i, j, 0))
    o2 = pl.BlockSpec((1, tt, 2 * d), lambda i, j: (i, j, 0))
    return pl.pallas_call(
        functools.partial(_prep_body, n_ctx=n_ctx, n_all=t),
        grid=(b, t // tt),
        in_specs=[pl.BlockSpec((1, tt, 3 * d), cur(0)),
                  pl.BlockSpec((1, 8, 3 * d), prev(0)),
                  pl.BlockSpec((1, 8, 3 * d), nxt(0)),
                  pl.BlockSpec((1, tt, wl), cur(lora_block)),
                  pl.BlockSpec((1, 8, wl), prev(lora_block)),
                  pl.BlockSpec((1, 8, wl), nxt(lora_block)),
                  pl.BlockSpec((1, 3 * d), const), pl.BlockSpec((1, wl), const),
                  pl.BlockSpec((1, d), const), pl.BlockSpec((1, 2 * d), const),
                  pl.BlockSpec((1, 2 * d), const), pl.BlockSpec((1, 2 * d), const),
                  pl.BlockSpec(wup2.shape, const), pl.BlockSpec(aup2.shape, const),
                  pl.BlockSpec(g_up.shape, const)],
        out_specs=[o1, o1, o1, o1, o2, o2, o2],
        out_shape=[one, one, one, one, both, both, both],
        compiler_params=_params("parallel", "parallel"),
        name=name,
    )(y, y, y, y, y, y, mu_rkv[None, :], mu_lora[None, :], k_k[None, :], jnp.tile(k_a, 2)[None, :],
      two(w0)[None, :], two(a0)[None, :], wup2, aup2, g_up)


def _post_body(yf_ref, yb_ref, r_ref, kd_ref, v_ref, g_ref, rk_ref, lg_ref, lb_ref, o_ref):
    d = o_ref.shape[2]
    inv = 1.0 / RWKV_HEAD_DIM
    wk = yf_ref[0] + yb_ref[0]
    cen = wk - _head_sums(wk, RWKV_HEAD_DIM) * inv
    var = _head_sums(cen * cen, RWKV_HEAD_DIM) * inv
    gn = cen * lax.rsqrt(var + RWKV_GN_EPS) * lg_ref[...] + lb_ref[...]
    ksum = kd_ref[0, :, :d] + kd_ref[0, :, d:]
    bonus = _head_sums(r_ref[0] * ksum * rk_ref[...], RWKV_HEAD_DIM) * v_ref[0]
    o_ref[0] = (gn + bonus) * g_ref[0]


def rwkv_post(y_f, y_b, r, kd2, v, g, r_k, lnx_g, lnx_b, name):
    b, t, d = r.shape
    tt = _row_tile(t, 256)
    blk = pl.BlockSpec((1, tt, d), lambda i, j: (i, j, 0))
    vec = pl.BlockSpec((1, d), lambda i, j: (0, 0))
    return pl.pallas_call(
        _post_body,
        grid=(b, t // tt),
        in_specs=[blk, blk, blk, pl.BlockSpec((1, tt, 2 * d), lambda i, j: (i, j, 0)), blk, blk, vec, vec, vec],
        out_specs=blk,
        out_shape=jax.ShapeDtypeStruct((b, t, d), F32),
        compiler_params=_params("parallel", "parallel"),
        name=name,
    )(y_f, y_b, r, kd2, v, g, r_k.reshape(1, d), lnx_g[None, :], lnx_b[None, :])


def _cumsum_rows(tri_bf16, x):
    hi = x.astype(BF16)
    r1 = x - hi.astype(F32)
    mid = r1.astype(BF16)
    lo = (r1 - mid.astype(F32)).astype(BF16)
    parts = jnp.concatenate([hi, mid, lo], axis=1)
    acc = jnp.dot(tri_bf16, parts, preferred_element_type=F32)
    n = x.shape[1]
    return acc[:, :n] + acc[:, n:2 * n] + acc[:, 2 * n:]


def _bdot(a, b):
    return jnp.dot(a, b, preferred_element_type=F32)


def _wkv_body(r_ref, v_ref, kk_ref, lw_ref, k_ref, a_ref, y_ref, s_ref, *, reverse, n_pairs):
    c = WKV_CHUNK
    w2 = 2 * c

    @pl.when(pl.program_id(1) == 0)
    def _():
        s_ref[...] = jnp.zeros_like(s_ref)

    row = lax.broadcasted_iota(jnp.int32, (w2, w2), 0)
    col = lax.broadcasted_iota(jnp.int32, (w2, w2), 1)
    row_a = row < c
    same = row_a == (col < c)
    tt = jnp.where(row_a, row, row - c)
    ss = jnp.where(col < c, col, col - c)
    before = (ss > tt) if reverse else (ss < tt)
    strict = same & before
    incl = same & (before | (ss == tt))
    r64 = lax.broadcasted_iota(jnp.int32, (c, c), 0)
    c64 = lax.broadcasted_iota(jnp.int32, (c, c), 1)
    tri = ((c64 >= r64) if reverse else (c64 <= r64)).astype(BF16)
    lane_a = lax.broadcasted_iota(jnp.int32, (c, w2), 1) < c
    last = 0 if reverse else c - 1
    pairs = range(n_pairs)
    sls = [slice(p * w2, (p + 1) * w2) for p in pairs]
    zero = jnp.zeros((c, w2), F32)
    zero2 = jnp.zeros((w2, w2), F32)

    def split_heads(x):
        return jnp.concatenate([jnp.where(lane_a, x, zero), jnp.where(lane_a, zero, x)], axis=0)

    lw = [lw_ref[0, :, sl] for sl in sls]
    big_l = [_cumsum_rows(tri, lw[p]) for p in pairs]
    l_tot = [big_l[p][last:last + 1, :] for p in pairs]

    kk = [kk_ref[0, :, sl] for sl in sls]
    kd = [k_ref[0, :, sl] for sl in sls]
    bb = [kk[p] * a_ref[0, :, sls[p]] for p in pairs]
    vv = [v_ref[0, :, sl] for sl in sls]
    e_neg = [jnp.exp(-big_l[p]) for p in pairs]
    q_t = [kk[p] * jnp.exp(big_l[p] - lw[p]) for p in pairs]
    r_t = [r_ref[0, :, sls[p]] * jnp.exp(big_l[p]) for p in pairs]
    k_t = [(kd[p] * e_neg[p]).astype(BF16) for p in pairs]
    b_t = [(bb[p] * e_neg[p]).astype(BF16) for p in pairs]
    e_rest = [jnp.exp(l_tot[p] - big_l[p]) for p in pairs]
    kb_p = [jnp.concatenate([kd[p] * e_rest[p], bb[p] * e_rest[p]], axis=0).astype(BF16) for p in pairs]

    lhs4 = [jnp.concatenate([split_heads(q_t[p]), split_heads(r_t[p])], axis=0).astype(BF16) for p in pairs]
    s_bd = [s_ref[p] for p in pairs]
    big = [lax.dot_general(lhs4[p],
                           jnp.concatenate([b_t[p], k_t[p], k_t[p], b_t[p], s_bd[p].astype(BF16)], axis=0),
                           (((1,), (1,)), ((), ())), preferred_element_type=F32) for p in pairs]
    n_bd, m_both, nr_bd = [], [], []
    for p in pairs:
        q1, r1 = big[p][:w2, :w2], big[p][w2:, :w2]
        q2, r2 = big[p][:w2, w2:2 * w2], big[p][w2:, w2:2 * w2]
        n_bd.append(jnp.where(strict, jnp.where(row_a, q1, q2), zero2).astype(BF16))
        nr_bd.append(jnp.where(incl, jnp.where(row_a, r1, r2), zero2).astype(BF16))
        m_both.append(jnp.concatenate([jnp.where(strict, jnp.where(row_a, q2, q1), zero2),
                                       jnp.where(incl, jnp.where(row_a, r2, r1), zero2)], axis=0).astype(BF16))
    mv = [_bdot(m_both[p], split_heads(vv[p]).astype(BF16)) for p in pairs]

    x = [big[p][:w2, 2 * w2:] + mv[p][:w2] for p in pairs]
    pw = n_bd
    span = 1
    while span < c:
        if 2 * span < c:
            both = [_bdot(pw[p], jnp.concatenate([x[p].astype(BF16), pw[p]], axis=1)) for p in pairs]
            pw = [both[p][:, w2:].astype(BF16) for p in pairs]
        else:
            both = [_bdot(pw[p], x[p].astype(BF16)) for p in pairs]
        x = [(x[p] - both[p][:, :w2]) if span == 1 else (x[p] + both[p][:, :w2]) for p in pairs]
        span *= 2
    u_bs = x

    nru = [_bdot(nr_bd[p], u_bs[p].astype(BF16)) for p in pairs]
    for p in pairs:
        y_bs = big[p][w2:, 2 * w2:] + mv[p][w2:] - nru[p]
        y_ref[0, :, sls[p]] = y_bs[:c] + y_bs[c:]

    vu = [jnp.concatenate([vv[p], -(u_bs[p][:c] + u_bs[p][c:])], axis=0).astype(BF16) for p in pairs]
    upd = [lax.dot_general(vu[p], kb_p[p], (((0,), (0,)), ((), ())), preferred_element_type=F32)
           for p in pairs]
    for p in pairs:
        s_ref[p] = s_bd[p] * jnp.exp(l_tot[p]) + jnp.where(same, upd[p], zero2)


def wkv_scan(r, v, kk, lw, kd, ad, *, n_ctx_chunks, reverse, name):
    b, t, d = r.shape
    c = WKV_CHUNK
    nc = t // c
    lc = n_ctx_chunks
    n_pairs = d // (2 * RWKV_HEAD_DIM)

    if reverse:
        def pos(s):
            return jnp.where(s < lc, lc - 1 - s, nc - 1 - s + lc)
    else:
        def pos(s):
            return s
    spec = pl.BlockSpec((1, c, d), lambda i, s: (i, pos(s), 0))
    side = pl.BlockSpec((1, c, d), lambda i, s: (i, pos(s), 1 if reverse else 0))
    return pl.pallas_call(
        functools.partial(_wkv_body, reverse=reverse, n_pairs=n_pairs),
        grid=(b, nc),
        in_specs=[spec, spec, spec, side, side, side],
        out_specs=spec,
        out_shape=jax.ShapeDtypeStruct((b, t, d), F32),
        scratch_shapes=[pltpu.VMEM((n_pairs, 2 * c, 2 * c), F32)],
        compiler_params=_params("parallel", "arbitrary"),
        name=name,
    )(r, v, kk, lw, kd, ad)


def _gelu(x):
    return 0.5 * x * (1.0 + lax.erf(x * (2.0 ** -0.5)))


def _gmlp_body(u_ref, v_ref, g_ref, b_ref, ws_ref, bs_ref, o_ref, *, groups):
    u = _gelu(u_ref[0])
    v = _gelu(v_ref[0])
    mu = jnp.mean(v, axis=-1, keepdims=True)
    var = jnp.mean(jnp.square(v - mu), axis=-1, keepdims=True)
    vn = (v - mu) * lax.rsqrt(var + GMLP_LN_EPS) * g_ref[...] + b_ref[...]
    gd = vn.shape[1] // groups
    for g in range(groups):
        sl = slice(g * gd, (g + 1) * gd)
        mixed = _dot(ws_ref[g], vn[:, sl]) + bs_ref[g]
        o_ref[0, :, sl] = u[:, sl] * mixed


def gmlp(y, u_block, v_block, ln_g, ln_b, ws, bs, name):
    b, t, _ = y.shape
    groups, chunk, _ = ws.shape
    dg = ln_g.shape[0]
    gd = dg // groups
    bs_b = jnp.broadcast_to(bs[:, :, None], (groups, chunk, gd))
    return pl.pallas_call(
        functools.partial(_gmlp_body, groups=groups),
        grid=(b, t // chunk),
        in_specs=[pl.BlockSpec((1, chunk, dg), lambda i, n: (i, n, u_block)),
                  pl.BlockSpec((1, chunk, dg), lambda i, n: (i, n, v_block)),
                  pl.BlockSpec((1, dg), lambda i, n: (0, 0)),
                  pl.BlockSpec((1, dg), lambda i, n: (0, 0)),
                  pl.BlockSpec((groups, chunk, chunk), lambda i, n: (0, 0, 0)),
                  pl.BlockSpec((groups, chunk, gd), lambda i, n: (0, 0, 0))],
        out_specs=pl.BlockSpec((1, chunk, dg), lambda i, n: (i, n, 0)),
        out_shape=jax.ShapeDtypeStruct((b, t, dg), F32),
        compiler_params=_params("parallel", "parallel"),
        name=name,
    )(y, y, ln_g[None, :], ln_b[None, :], ws, bs_b)


def _norm_rope(x, gain, cos, sin):
    x = x * lax.rsqrt(jnp.mean(x * x, axis=-1, keepdims=True) + NORM_EPS) * gain
    lane = lax.broadcasted_iota(jnp.int32, x.shape, 1)
    quarter = ATT_HEAD_DIM // 4
    partner = jnp.where((lane & quarter) == 0, pltpu.roll(x, ATT_HEAD_DIM - quarter, axis=1),
                        pltpu.roll(x, quarter, axis=1))
    return x * cos + partner * sin


def _kprep_body(k_ref, g_ref, cos_ref, sin_ref, o_ref):
    hd = ATT_HEAD_DIM
    for h in range(k_ref.shape[2] // hd):
        sl = slice(h * hd, (h + 1) * hd)
        o_ref[0, :, sl] = _norm_rope(k_ref[0, :, sl], g_ref[...], cos_ref[...], sin_ref[...]).astype(BF16)


def key_prep(y, k_block, k_g, cos, sin, n_kv, name):
    b, t, _ = y.shape
    w = n_kv * ATT_HEAD_DIM
    tt = _row_tile(t, 256)
    tab = pl.BlockSpec((tt, ATT_HEAD_DIM), lambda i, j: (j, 0))
    return pl.pallas_call(
        _kprep_body,
        grid=(b, t // tt),
        in_specs=[pl.BlockSpec((1, tt, w), lambda i, j: (i, j, k_block)),
                  pl.BlockSpec((1, ATT_HEAD_DIM), lambda i, j: (0, 0)), tab, tab],
        out_specs=pl.BlockSpec((1, tt, w), lambda i, j: (i, j, 0)),
        out_shape=jax.ShapeDtypeStruct((b, t, w), BF16),
        compiler_params=_params("parallel", "parallel"),
        name=name,
    )(y, k_g[None, :], cos, sin)


def _attn_body(q_ref, k_ref, v_ref, g_ref, cos_ref, sin_ref, o_ref, *, group):
    k = k_ref[0]
    v = v_ref[0].astype(BF16)
    hd = ATT_HEAD_DIM
    scale = hd ** -0.5
    for g in range(group):
        sl = slice(g * hd, (g + 1) * hd)
        q = _norm_rope(q_ref[0, :, sl], g_ref[...], cos_ref[...], sin_ref[...]) * scale
        s = _dot_nt(q, k)
        p = jnp.exp(s - jnp.max(s, axis=-1, keepdims=True))
        denom = jnp.sum(p, axis=-1, keepdims=True)
        o_ref[0, :, sl] = jnp.dot(p.astype(BF16), v, preferred_element_type=F32) / denom


def attention(y, kb, q_g, cos, sin, *, n_q, n_kv, n_ctx, name):
    b, t, _ = y.shape
    hd = ATT_HEAD_DIM
    group = n_q // n_kv
    tq = t - n_ctx
    bq = _row_tile(n_ctx, 256)
    assert tq % bq == 0
    off = n_ctx // bq
    tab = pl.BlockSpec((bq, hd), lambda i, h, j: (off + j, 0))
    return pl.pallas_call(
        functools.partial(_attn_body, group=group),
        grid=(b, n_kv, tq // bq),
        in_specs=[pl.BlockSpec((1, bq, group * hd), lambda i, h, j: (i, off + j, h)),
                  pl.BlockSpec((1, t, hd), lambda i, h, j: (i, 0, h)),
                  pl.BlockSpec((1, t, hd), lambda i, h, j: (i, 0, n_q + n_kv + h)),
                  pl.BlockSpec((1, hd), lambda i, h, j: (0, 0)), tab, tab],
        out_specs=pl.BlockSpec((1, bq, group * hd), lambda i, h, j: (i, j, h)),
        out_shape=jax.ShapeDtypeStruct((b, tq, n_q * hd), F32),
        compiler_params=_params("parallel", "parallel", "arbitrary"),
        name=name,
    )(y, kb, y, q_g[None, :], cos, sin)


def _router_body(h_ref, w_ref, b_ref, idx_ref, wt_ref, pos_ref, cnt_ref, run_ref):
    logits = jnp.dot(h_ref[...], w_ref[...], precision=lax.Precision.HIGHEST,
                     preferred_element_type=F32) + b_ref[...]
    n_e = logits.shape[1]
    lane = lax.broadcasted_iota(jnp.int32, logits.shape, 1)
    vals, ids = [], []
    cur = logits
    for _ in range(TOP_K):
        m = jnp.max(cur, axis=1, keepdims=True)
        sel = jnp.min(jnp.where(cur == m, lane, n_e), axis=1, keepdims=True)
        vals.append(m)
        ids.append(sel)
        cur = jnp.where(lane == sel, -jnp.inf, cur)
    e = jnp.exp(jnp.concatenate(vals, axis=1) - vals[0])
    idx_ref[...] = jnp.concatenate(ids, axis=1)
    wt_ref[...] = e / jnp.sum(e, axis=1, keepdims=True)

    @pl.when(pl.program_id(0) == 0)
    def _():
        run_ref[...] = jnp.zeros_like(run_ref)

    tm = logits.shape[0]
    hot = [jnp.where(lane == sel, 1.0, 0.0) for sel in ids]
    chosen = hot[0] + hot[1] + hot[2] + hot[3]
    tri = jnp.where(lax.broadcasted_iota(jnp.int32, (tm, tm), 1) < lax.broadcasted_iota(jnp.int32, (tm, tm), 0),
                    1.0, 0.0).astype(BF16)
    base = run_ref[...] + jnp.dot(tri, chosen.astype(BF16), preferred_element_type=F32)
    pos_ref[...] = jnp.concatenate([jnp.sum(h * base, axis=1, keepdims=True) for h in hot],
                                   axis=1).astype(jnp.int32)
    run_ref[...] += jnp.sum(chosen, axis=0, keepdims=True)
    cnt_ref[...] = run_ref[...]


def router(h, rw, rb, name):
    t, d = h.shape
    n_e = rw.shape[1]
    tm = _row_tile(t, 256)
    return pl.pallas_call(
        _router_body,
        grid=(t // tm,),
        in_specs=[pl.BlockSpec((tm, d), lambda i: (i, 0)),
                  pl.BlockSpec((d, n_e), lambda i: (0, 0)),
                  pl.BlockSpec((1, n_e), lambda i: (0, 0))],
        out_specs=[pl.BlockSpec((tm, TOP_K), lambda i: (i, 0)),
                   pl.BlockSpec((tm, TOP_K), lambda i: (i, 0)),
                   pl.BlockSpec((tm, TOP_K), lambda i: (i, 0)),
                   pl.BlockSpec((1, n_e), lambda i: (0, 0))],
        out_shape=[jax.ShapeDtypeStruct((t, TOP_K), jnp.int32),
                   jax.ShapeDtypeStruct((t, TOP_K), F32),
                   jax.ShapeDtypeStruct((t, TOP_K), jnp.int32),
                   jax.ShapeDtypeStruct((1, n_e), F32)],
        scratch_shapes=[pltpu.VMEM((1, n_e), F32)],
        compiler_params=_params("arbitrary"),
        name=name,
    )(h, rw, rb[None, :])


def _dispatch_body(dest_ref, x_ref, init_ref, xs_ref, sem, *, rows):
    del init_ref

    def row_copy(r, j):
        return pltpu.make_async_copy(x_ref.at[pl.ds(r, 1)], xs_ref.at[pl.ds(dest_ref[0, 0, r * TOP_K + j], 1)], sem)

    def issue(r, carry):
        for j in range(TOP_K):
            row_copy(r, j).start()
        return carry

    def drain(r, carry):
        for j in range(TOP_K):
            row_copy(r, j).wait()
        return carry

    lax.fori_loop(0, rows, issue, 0, unroll=8)
    lax.fori_loop(0, rows, drain, 0, unroll=8)


def dispatch(tokens, dest, np_rows, name):
    t, d = tokens.shape
    rows = _row_tile(t, 256)
    n_tiles = t // rows
    return pl.pallas_call(
        functools.partial(_dispatch_body, rows=rows),
        grid=(n_tiles,),
        in_specs=[pl.BlockSpec((1, 1, rows * TOP_K), lambda i: (i, 0, 0), memory_space=pltpu.SMEM),
                  pl.BlockSpec((rows, d), lambda i: (i, 0)),
                  pl.BlockSpec(memory_space=pl.ANY)],
        out_specs=pl.BlockSpec(memory_space=pl.ANY),
        out_shape=jax.ShapeDtypeStruct((np_rows, d), F32),
        scratch_shapes=[pltpu.SemaphoreType.DMA],
        input_output_aliases={2: 0},
        compiler_params=_params("arbitrary"),
        name=name,
    )(dest.reshape(n_tiles, 1, rows * TOP_K), tokens, jnp.zeros((np_rows, d), F32))


def _combine_body(dest_ref, w_ref, ys_ref, o_ref, buf_ref, sem, *, rows):
    def row_copy(r, j):
        return pltpu.make_async_copy(ys_ref.at[pl.ds(dest_ref[0, 0, r * TOP_K + j], 1)],
                                     buf_ref.at[j, pl.ds(r, 1)], sem)

    def issue(r, carry):
        for j in range(TOP_K):
            row_copy(r, j).start()
        return carry

    def drain(r, carry):
        for j in range(TOP_K):
            row_copy(r, j).wait()
        return carry

    lax.fori_loop(0, rows, issue, 0, unroll=8)
    lax.fori_loop(0, rows, drain, 0, unroll=8)
    acc = w_ref[:, 0:1] * buf_ref[0]
    for j in range(1, TOP_K):
        acc = acc + w_ref[:, j:j + 1] * buf_ref[j]
    o_ref[...] = acc


def combine(ys, dest, wts, name):
    t = dest.shape[0]
    d = ys.shape[1]
    rows = _row_tile(t, 256)
    n_tiles = t // rows
    return pl.pallas_call(
        functools.partial(_combine_body, rows=rows),
        grid=(n_tiles,),
        in_specs=[pl.BlockSpec((1, 1, rows * TOP_K), lambda i: (i, 0, 0), memory_space=pltpu.SMEM),
                  pl.BlockSpec((rows, TOP_K), lambda i: (i, 0)),
                  pl.BlockSpec(memory_space=pl.ANY)],
        out_specs=pl.BlockSpec((rows, d), lambda i: (i, 0)),
        out_shape=jax.ShapeDtypeStruct((t, d), F32),
        scratch_shapes=[pltpu.VMEM((TOP_K, rows, d), F32), pltpu.SemaphoreType.DMA],
        compiler_params=_params("arbitrary"),
        name=name,
    )(dest.reshape(n_tiles, 1, rows * TOP_K), wts, ys)


def _ffn_body(te_ref, nv_ref, x_ref, wg_ref, wl_ref, bg_ref, bl_ref, w2_ref, b2_ref, o_ref, xb_ref):
    i = pl.program_id(0)
    c = pl.program_id(1)
    valid = i < nv_ref[0]

    @pl.when(valid & (c == 0))
    def _():
        xb_ref[...] = x_ref[...].astype(BF16)
        o_ref[...] = jnp.broadcast_to(b2_ref[0], o_ref.shape)

    @pl.when(valid)
    def _():
        x = xb_ref[...]
        glu = _dot(x, wg_ref[0]) + bg_ref[0]
        lin = _dot(x, wl_ref[0]) + bl_ref[0]
        glu = jnp.minimum(glu, SWIGLU_LIMIT)
        lin = jnp.clip(lin, -SWIGLU_LIMIT, SWIGLU_LIMIT)
        act = glu * jax.nn.sigmoid(SWIGLU_ALPHA * glu) * (lin + 1.0)
        o_ref[...] += _dot(act, w2_ref[0])

    @pl.when(jnp.logical_not(valid) & (c == 0))
    def _():
        o_ref[...] = jnp.zeros_like(o_ref)


def moe_ffn(xs, tile_expert, n_valid, w1, b1, w2, b2, *, layer, tm, fc, name):
    np_rows, d = xs.shape
    f = w2.shape[2]
    nfc = f // fc
    n_tiles = np_rows // tm
    grid_spec = pltpu.PrefetchScalarGridSpec(
        num_scalar_prefetch=2,
        grid=(n_tiles, nfc),
        in_specs=[pl.BlockSpec((tm, d), lambda i, c, te, nv: (i, 0)),
                  pl.BlockSpec((None, 1, d, fc), lambda i, c, te, nv: (layer, te[i], 0, c)),
                  pl.BlockSpec((None, 1, d, fc), lambda i, c, te, nv: (layer, te[i], 0, nfc + c)),
                  pl.BlockSpec((1, 1, fc), lambda i, c, te, nv: (te[i], 0, c)),
                  pl.BlockSpec((1, 1, fc), lambda i, c, te, nv: (te[i], 0, nfc + c)),
                  pl.BlockSpec((None, 1, fc, d), lambda i, c, te, nv: (layer, te[i], c, 0)),
                  pl.BlockSpec((1, 1, d), lambda i, c, te, nv: (te[i], 0, 0))],
        out_specs=pl.BlockSpec((tm, d), lambda i, c, te, nv: (i, 0)),
        scratch_shapes=[pltpu.VMEM((tm, d), BF16)],
    )
    return pl.pallas_call(
        _ffn_body,
        grid_spec=grid_spec,
        out_shape=jax.ShapeDtypeStruct((np_rows, d), F32),
        compiler_params=_params("arbitrary", "arbitrary"),
        name=name,
    )(tile_expert, n_valid, xs, w1, w1, b1, b1, w2, b2)


def moe(tokens, rw, rb, w1, b1, w2, b2, layer, tag):
    t, d = tokens.shape
    n_e = rw.shape[1]
    tm = 1024 if t * TOP_K >= 1024 * n_e else 128
    fc = 256
    idx, wts, pos, cnt = router(tokens, rw, rb, "router" + tag)

    n_pairs = t * TOP_K
    counts = cnt[0].astype(jnp.int32)
    padded = (counts + tm - 1) // tm * tm
    ends_p = jnp.cumsum(padded)
    starts_p = ends_p - padded
    e_ids = jnp.arange(n_e, dtype=jnp.int32)
    dest = pos + jnp.sum(jnp.where(idx[:, :, None] == e_ids, starts_p, 0), axis=-1)
    n_tiles = -(-n_pairs // tm) + n_e
    np_rows = n_tiles * tm
    n_valid = (ends_p[-1] // tm).astype(jnp.int32)
    tile_start = jnp.arange(n_tiles, dtype=jnp.int32) * tm
    tile_expert = jnp.sum((tile_start[:, None] >= ends_p[None, :]).astype(jnp.int32), axis=1)
    last_e = jnp.sum((ends_p[-1] - 1 >= ends_p).astype(jnp.int32))
    tile_expert = jnp.minimum(tile_expert, last_e)

    xs = dispatch(tokens, dest, np_rows, "dispatch" + tag)
    ys = moe_ffn(xs, tile_expert, n_valid[None], w1, b1[:, None, :], w2, b2[:, None, :], layer=layer,
                 tm=tm, fc=fc, name="moe_ffn" + tag)
    return combine(ys, dest, wts, "combine" + tag)


def _rmsnorm(x, g):
    return x * lax.rsqrt(jnp.mean(x * x, axis=-1, keepdims=True) + NORM_EPS) * g


def _adaln(cvec, w_mod, b_mod, name):
    m = project(jax.nn.silu(cvec), w_mod, name) + b_mod
    return jnp.split(m, N_MOD, axis=-1)


def _even_layout(d_rwkv, lora_w, lora_a, lora_g, d_gmlp):
    pad = lambda n: -(-n // LANES) * LANES
    d_shift = 3 * d_rwkv + lora_w + lora_a + lora_g
    src = {"rkv": (0, 3 * d_rwkv), "xw": (3 * d_rwkv, lora_w), "xa": (3 * d_rwkv + lora_w, lora_a),
           "xg": (3 * d_rwkv + lora_w + lora_a, lora_g), "gmlp": (d_shift, 2 * d_gmlp)}
    order = ["rkv", "gmlp", "xw", "xa", "xg"]
    dst, off = {}, 0
    for name in order:
        dst[name] = off
        off += pad(src[name][1])
    return src, dst, order, off


def _regroup_cols(w, src, dst, order, total):
    parts = []
    for name in order:
        s, n = src[name]
        blk = w[..., s:s + n]
        padn = -(-n // LANES) * LANES - n
        if padn:
            blk = jnp.pad(blk, [(0, 0)] * (w.ndim - 1) + [(0, padn)])
        parts.append(blk)
    return jnp.concatenate(parts, axis=-1)


def _even_mixer(nx, nc, p, need_ctx, tag):
    (w_in, mu, w0, w_up, a0, a_up, g_up, k_k, k_a, r_k, lnx_g, lnx_b, v_ln_g, v_ln_b, ws, bs, w_out) = p
    b, s, d = nx.shape
    l = nc.shape[1]
    d_rwkv = g_up.shape[1]
    lora_w, lora_a, lora_g = w_up.shape[1], a_up.shape[1], g_up.shape[0]
    d_gmlp = v_ln_g.shape[0]
    src, dst, order, total = _even_layout(d_rwkv, lora_w, lora_a, lora_g, d_gmlp)
    w_in_r = _regroup_cols(w_in, src, dst, order, total)
    mu_r = _regroup_cols(jnp.concatenate([mu, jnp.zeros((2 * d_gmlp,), F32)]), src, dst, order, total)

    t = l + s
    n_all = jnp.concatenate([nc, nx], axis=1)
    y = project(n_all.reshape(b * t, d), w_in_r, "even_in" + tag).reshape(b, t, total)

    wl = total - dst["xw"]
    assert dst["xw"] % wl == 0 and dst["gmlp"] % d_gmlp == 0
    r, v, kk, g, lw2, kd2, ad2 = rwkv_prep(y, dst["xw"] // wl, mu_r[:3 * d_rwkv], mu_r[dst["xw"]:], k_k, k_a,
                                           w0, a0, w_up, a_up, g_up, n_ctx=l, name="rwkv_prep" + tag)
    y_dir = [wkv_scan(r, v, kk, lw2, kd2, ad2, n_ctx_chunks=l // WKV_CHUNK, reverse=dd == 1,
                      name="wkv%d%s" % (dd, tag)) for dd in range(2)]
    rw_out = rwkv_post(y_dir[0], y_dir[1], r, kd2, v, g, r_k, lnx_g, lnx_b, "rwkv_post" + tag)

    ub = dst["gmlp"] // d_gmlp
    g_out = gmlp(y, ub, ub + 1, v_ln_g, v_ln_b, ws, bs, "gmlp" + tag)
    mix = matmul2(rw_out.reshape(b * t, d_rwkv), g_out.reshape(b * t, d_gmlp), w_out,
                  "even_out" + tag).reshape(b, t, d)
    return mix[:, l:], (mix[:, :l] if need_ctx else None)


def _rope_tables(n_ctx, rows):
    row = jnp.repeat(jnp.arange(rows, dtype=F32), GRID_W)
    col = jnp.tile(jnp.arange(GRID_W, dtype=F32), rows)
    n_freq = ATT_HEAD_DIM // 4
    inv = jnp.power(jnp.float32(ROPE_THETA), -jnp.arange(n_freq, dtype=F32) / n_freq)
    ang_r = row[:, None] * inv
    ang_c = col[:, None] * inv
    cos = jnp.concatenate([jnp.cos(ang_r), jnp.cos(ang_r), jnp.cos(ang_c), jnp.cos(ang_c)], axis=1)
    sin = jnp.concatenate([-jnp.sin(ang_r), jnp.sin(ang_r), -jnp.sin(ang_c), jnp.sin(ang_c)], axis=1)
    cos = jnp.concatenate([jnp.ones((n_ctx, ATT_HEAD_DIM), F32), cos], axis=0)
    sin = jnp.concatenate([jnp.zeros((n_ctx, ATT_HEAD_DIM), F32), sin], axis=0)
    return cos, sin


def _odd_mixer(nx, nc, p, rope, need_ctx, tag):
    w_in, q_g, k_g, w_out = p
    b, s, d = nx.shape
    l = nc.shape[1]
    hd = ATT_HEAD_DIM
    n_q = w_out.shape[0] // hd
    n_kv = (w_in.shape[1] // hd - n_q) // 2
    assert not need_ctx, "context-query attention is only needed when an attention layer is not the last layer"
    assert n_q % n_kv == 0
    cos, sin = rope
    t = l + s
    n_all = jnp.concatenate([nc, nx], axis=1)
    y = project(n_all.reshape(b * t, d), w_in, "odd_in" + tag).reshape(b, t, -1)
    kb = key_prep(y, n_q // n_kv, k_g, cos, sin, n_kv, "key_prep" + tag)
    ox = attention(y, kb, q_g, cos, sin, n_q=n_q, n_kv=n_kv, n_ctx=l, name="attn" + tag)
    mix_x = project(ox.reshape(b * s, -1), w_out, "odd_out" + tag).reshape(b, s, d)
    return mix_x, None


def kernel(x, c, ctx, c_ctx, norm1_g, norm2_g, w_mod, b_mod, e_w_in, e_mu, e_w0, e_w_up, e_a0, e_a_up, e_g_up, e_k_k, e_k_a, e_r_k, e_lnx_g, e_lnx_b, e_v_ln_g, e_v_ln_b, e_ws, e_bs, e_w_out, o_w_in, o_q_g, o_k_g, o_w_out, router_w, router_b, moe_w1, moe_b1, moe_w2, moe_b2, final_g):
    b, s, d = x.shape
    l = ctx.shape[1]
    depth = norm1_g.shape[0]
    rope = _rope_tables(l, s // GRID_W)
    h_lat, h_ctx = x, ctx
    for i in range(depth):
        tag = "_l%d" % i
        need_ctx = i < depth - 1
        j = i // 2
        mods = _adaln(jnp.concatenate([c, c_ctx[None, :]], axis=0), w_mod[i], b_mod[i], "adaln" + tag)
        m_lat = [m[:b, None, :] for m in mods]
        m_ctx = [m[b:, None, :] for m in mods]
        nx = _rmsnorm(h_lat, norm1_g[i]) * (1.0 + m_lat[1]) + m_lat[0]
        nc = _rmsnorm(h_ctx, norm1_g[i]) * (1.0 + m_ctx[1]) + m_ctx[0]
        if i % 2 == 0:
            p = (e_w_in[j], e_mu[j], e_w0[j], e_w_up[j], e_a0[j], e_a_up[j], e_g_up[j], e_k_k[j], e_k_a[j],
                 e_r_k[j], e_lnx_g[j], e_lnx_b[j], e_v_ln_g[j], e_v_ln_b[j], e_ws[j], e_bs[j], e_w_out[j])
            mx, mc = _even_mixer(nx, nc, p, need_ctx, tag)
        else:
            mx, mc = _odd_mixer(nx, nc, (o_w_in[j], o_q_g[j], o_k_g[j], o_w_out[j]), rope, need_ctx, tag)
        h_lat = h_lat + m_lat[2] * mx
        nx = _rmsnorm(h_lat, norm2_g[i]) * (1.0 + m_lat[4]) + m_lat[3]
        if need_ctx:
            h_ctx = h_ctx + m_ctx[2] * mc
            nc = _rmsnorm(h_ctx, norm2_g[i]) * (1.0 + m_ctx[4]) + m_ctx[3]
            tokens = jnp.concatenate([nx.reshape(b * s, d), nc.reshape(b * l, d)], axis=0)
        else:
            tokens = nx.reshape(b * s, d)
        f = moe(tokens, router_w[i], router_b[i], moe_w1, moe_b1[i], moe_w2, moe_b2[i], i, tag)
        h_lat = h_lat + m_lat[5] * f[:b * s].reshape(b, s, d)
        if need_ctx:
            h_ctx = h_ctx + m_ctx[5] * f[b * s:].reshape(b, l, d)
    return _rmsnorm(h_lat, final_g)
```
